```python
import math
import jax
import jax.numpy as jnp
from jax import lax
import numpy as np

D_MODEL = 2048
BATCH = 16
SEQ = 2048
DEPTH = 2
DEC_BATCH = 2
DEC_SEQ = 16384
PAST_LEN = 128

HEAD_DIM = 64
A_HEADS = 12
A_Q_LORA = 384
A_KV_LORA = 256
A_NOPE = 64
A_ROPE = 32
B_HEADS = 12
B_KV_HEADS = 4
B_WINDOW = 128
C_HEADS = 12
DILATED_PAIRS = ((128, 1), (512, 4), (2048, 16))
D_HEADS = 12
D_KV_HEADS = 4
GROUP_WIDTH = A_HEADS * HEAD_DIM
D_MIX = 4 * GROUP_WIDTH
A_COLS = A_Q_LORA + A_KV_LORA + A_ROPE
B_COLS = (B_HEADS + 2 * B_KV_HEADS) * HEAD_DIM
C_COLS = 3 * C_HEADS * HEAD_DIM
D_COLS = (D_HEADS + 2 * D_KV_HEADS) * HEAD_DIM
N_IN = A_COLS + B_COLS + C_COLS + D_COLS + D_MIX
SPLITS = (A_COLS, A_COLS + B_COLS, A_COLS + B_COLS + C_COLS, A_COLS + B_COLS + C_COLS + D_COLS)
GRID_W = 64
Q_BLOCK = 128
NUM_BUCKETS = 32
T5_MAX_DISTANCE = 1024
ROPE_THETA = 10000.0
EPS = 1e-6
NEG_INF = -1e30
ALPHA = (2 * DEPTH) ** 0.25
BETA = (8 * DEPTH) ** -0.25

kernel_name = "hybrid_parallel_head_encoder"


def layer_norm(x, g=None, b=None):
    xf = x.astype(jnp.float32)
    mu = jnp.mean(xf, axis=-1, keepdims=True)
    var = jnp.mean(jnp.square(xf - mu), axis=-1, keepdims=True)
    y = (xf - mu) * lax.rsqrt(var + EPS)
    if g is not None:
        y = y * g.astype(jnp.float32) + b.astype(jnp.float32)
    return y.astype(x.dtype)


def rms_norm(x, g):
    xf = x.astype(jnp.float32)
    y = xf * lax.rsqrt(jnp.mean(jnp.square(xf), axis=-1, keepdims=True) + EPS)
    return (y * g.astype(jnp.float32)).astype(x.dtype)


def rope_angles(pos, dim):
    inv = ROPE_THETA ** (-jnp.arange(0, dim, 2, dtype=jnp.float32) / dim)
    ang = pos.astype(jnp.float32)[:, None] * inv[None, :]
    return jnp.cos(ang), jnp.sin(ang)


def apply_rope(x, cos, sin):
    half = x.shape[-1] // 2
    xf = x.astype(jnp.float32)
    x1, x2 = xf[..., :half], xf[..., half:]
    return jnp.concatenate([x1 * cos - x2 * sin, x1 * sin + x2 * cos], axis=-1).astype(x.dtype)


def axial_rope(x, row, col):
    half = HEAD_DIM // 2
    cr, sr = rope_angles(row, half)
    cc, sc = rope_angles(col, half)
    return jnp.concatenate([apply_rope(x[..., :half], cr[:, None], sr[:, None]),
                            apply_rope(x[..., half:], cc[:, None], sc[:, None])], axis=-1)


def t5_bucket(rel):
    half = NUM_BUCKETS // 2
    exact = half // 2
    n = jnp.abs(rel)
    large = exact + (jnp.log(jnp.maximum(n, 1).astype(jnp.float32) / exact)
                     / math.log(T5_MAX_DISTANCE / exact) * (half - exact)).astype(jnp.int32)
    large = jnp.minimum(large, half - 1)
    return jnp.where(rel > 0, half, 0) + jnp.where(n < exact, n, large)


def relative_bias(table, rel):
    return jnp.transpose(table[t5_bucket(rel)], (2, 0, 1)).astype(jnp.float32)


def band_offsets(w):
    return jnp.arange(3 * w)[None, :] - w - jnp.arange(w)[:, None]


def dense_attention(q, k, v):
    B, S, H, dk = q.shape
    Hkv = k.shape[2]
    G = H // Hkv
    dv = v.shape[-1]
    nb = S // Q_BLOCK
    qb = q.reshape(B, nb, Q_BLOCK, Hkv, G, dk).transpose(1, 0, 2, 3, 4, 5)
    scale = dk ** -0.5

    def one_block(qi):
        s = jnp.einsum("bqhgd,bkhd->bhgqk", qi, k, preferred_element_type=jnp.float32) * scale
        p = jax.nn.softmax(s, axis=-1).astype(v.dtype)
        return jnp.einsum("bhgqk,bkhd->bqhgd", p, v)

    o = lax.map(one_block, qb)
    return o.transpose(1, 0, 2, 3, 4, 5).reshape(B, S, H, dv)


def banded_attention(q, k, v, bias, w, sink):
    B, L, H, dk = q.shape
    Hkv = k.shape[2]
    G = H // Hkv
    dv = v.shape[-1]
    nb = -(-L // w)
    pad = nb * w - L
    qb = jnp.pad(q, ((0, 0), (0, pad), (0, 0), (0, 0))).reshape(B, nb, w, Hkv, G, dk)

    def windows(t):
        tp = jnp.pad(t, ((0, 0), (w, w + pad), (0, 0), (0, 0))).reshape(B, nb + 2, w, Hkv, t.shape[-1])
        return jnp.concatenate([tp[:, :-2], tp[:, 1:-1], tp[:, 2:]], axis=2)

    kw, vw = windows(k), windows(v)
    kpos = jnp.arange(-w, (nb + 1) * w).reshape(nb + 2, w)
    kvalid = (kpos >= 0) & (kpos < L)
    kvalid_w = jnp.concatenate([kvalid[:-2], kvalid[1:-1], kvalid[2:]], axis=1)
    mask = (jnp.abs(band_offsets(w)) <= w)[None] & kvalid_w[:, None, :]
    s = jnp.einsum("bnqhgd,bnkhd->bnhgqk", qb, kw, preferred_element_type=jnp.float32) * (dk ** -0.5)
    s = s + bias.reshape(Hkv, G, w, 3 * w)
    s = jnp.where(mask[None, :, None, None], s, NEG_INF)
    m = jnp.max(s, axis=-1)
    if sink is not None:
        sk = sink.astype(jnp.float32).reshape(1, 1, Hkv, G, 1)
        m = jnp.maximum(m, sk)
    e = jnp.exp(s - m[..., None])
    den = jnp.sum(e, axis=-1)
    if sink is not None:
        den = den + jnp.exp(sk - m)
    p = (e / den[..., None]).astype(v.dtype)
    o = jnp.einsum("bnhgqk,bnkhd->bnqhgd", p, vw).reshape(B, nb * w, H, dv)[:, :L]
    lse = (m + jnp.log(den)).transpose(0, 1, 4, 2, 3).reshape(B, nb * w, H)[:, :L]
    return o, lse


def to_strided(t, r):
    B, S = t.shape[:2]
    rest = t.shape[2:]
    t = jnp.moveaxis(t.reshape((B, S // r, r) + rest), 2, 1)
    return t.reshape((B * r, S // r) + rest)


def from_strided(t, B, r):
    Sr = t.shape[1]
    rest = t.shape[2:]
    t = jnp.moveaxis(t.reshape((B, r, Sr) + rest), 1, 2)
    return t.reshape((B, Sr * r) + rest)


def mla_mixer(cols, pos, q_norm, q_up, kv_norm, kv_up):
    B, S, _ = cols.shape
    q_lat, kv_lat, k_pe = jnp.split(cols, [A_Q_LORA, A_Q_LORA + A_KV_LORA], axis=-1)
    q = (rms_norm(q_lat, q_norm) @ q_up).reshape(B, S, A_HEADS, A_NOPE + A_ROPE)
    kv = (rms_norm(kv_lat, kv_norm) @ kv_up).reshape(B, S, A_HEADS, A_NOPE + HEAD_DIM)
    cos, sin = rope_angles(pos, A_ROPE)
    q = jnp.concatenate([q[..., :A_NOPE], apply_rope(q[..., A_NOPE:], cos[:, None], sin[:, None])], axis=-1)
    k_pe = apply_rope(k_pe, cos, sin)
    k = jnp.concatenate([kv[..., :A_NOPE],
                         jnp.broadcast_to(k_pe[:, :, None, :], (B, S, A_HEADS, A_ROPE))], axis=-1)
    v = kv[..., A_NOPE:]
    return dense_attention(q, k, v).reshape(B, S, GROUP_WIDTH)


def window_mixer(cols, sink, table):
    B, S, _ = cols.shape
    q, k, v = jnp.split(cols, [B_HEADS * HEAD_DIM, (B_HEADS + B_KV_HEADS) * HEAD_DIM], axis=-1)
    q = q.reshape(B, S, B_HEADS, HEAD_DIM)
    k = k.reshape(B, S, B_KV_HEADS, HEAD_DIM)
    v = v.reshape(B, S, B_KV_HEADS, HEAD_DIM)
    bias = relative_bias(table, band_offsets(B_WINDOW))
    o, _ = banded_attention(q, k, v, bias, B_WINDOW, sink)
    return o.reshape(B, S, GROUP_WIDTH)


def dilated_mixer(cols, table):
    B, S, _ = cols.shape
    q, k, v = [t.reshape(B, S, C_HEADS, HEAD_DIM) for t in jnp.split(cols, 3, axis=-1)]
    outs, lses = [], []
    for window, r in DILATED_PAIRS:
        w = window // (2 * r)
        bias = relative_bias(table, r * band_offsets(w))
        o, lse = banded_attention(to_strided(q, r), to_strided(k, r), to_strided(v, r), bias, w, None)
        outs.append(from_strided(o, B, r))
        lses.append(from_strided(lse, B, r))
    alpha = jax.nn.softmax(jnp.stack(lses), axis=0)
    o = jnp.sum(alpha[..., None] * jnp.stack(outs).astype(jnp.float32), axis=0).astype(cols.dtype)
    return o.reshape(B, S, GROUP_WIDTH)


def axial_mixer(cols, row, col, q_norm, k_norm):
    B, S, _ = cols.shape
    q, k, v = jnp.split(cols, [D_HEADS * HEAD_DIM, (D_HEADS + D_KV_HEADS) * HEAD_DIM], axis=-1)
    q = axial_rope(rms_norm(q.reshape(B, S, D_HEADS, HEAD_DIM), q_norm), row, col)
    k = axial_rope(rms_norm(k.reshape(B, S, D_KV_HEADS, HEAD_DIM), k_norm), row, col)
    v = v.reshape(B, S, D_KV_HEADS, HEAD_DIM)
    return dense_attention(q, k, v).reshape(B, S, GROUP_WIDTH)


def hybrid_layer(x, c, w_ada, b_ada, w_in, a_q_norm, a_q_up, a_kv_norm, a_kv_up, b_sink,
                 d_q_norm, d_k_norm, w_out, ln_g, ln_b, rel_bias):
    B, S, _ = x.shape
    rows = S // GRID_W
    pos = jnp.arange(S)
    row = jnp.repeat(jnp.arange(rows), GRID_W)
    col = jnp.tile(jnp.arange(GRID_W), rows)
    mod = jax.nn.silu(c) @ w_ada + b_ada
    shift, scale, gate = jnp.split(mod[:, None, :], 3, axis=-1)
    h = layer_norm(x) * (1 + scale) + shift
    proj = h @ w_in
    a_in, b_in, c_in, d_in, z = jnp.split(proj, SPLITS, axis=-1)
    y = jnp.concatenate([
        mla_mixer(a_in, pos, a_q_norm, a_q_up, a_kv_norm, a_kv_up),
        window_mixer(b_in, b_sink, rel_bias[:, :B_HEADS]),
        dilated_mixer(c_in, rel_bias[:, B_HEADS:]),
        axial_mixer(d_in, row, col, d_q_norm, d_k_norm),
    ], axis=-1)
    y = (y * jax.nn.silu(z)) @ w_out
    return layer_norm(ALPHA * x + gate * y, ln_g, ln_b)


def trunk(x, c, w_ada, b_ada, w_in, a_q_norm, a_q_up, a_kv_norm, a_kv_up, b_sink,
          d_q_norm, d_k_norm, w_out, ln_g, ln_b, rel_bias):
    for l in range(DEPTH):
        x = hybrid_layer(x, c, w_ada[l], b_ada[l], w_in[l], a_q_norm[l], a_q_up[l], a_kv_norm[l],
                         a_kv_up[l], b_sink[l], d_q_norm[l], d_k_norm[l], w_out[l], ln_g[l], ln_b[l],
                         rel_bias)
    return x


def _normal(k, shape, s):
    return s * jax.random.normal(k, shape, jnp.float32)


def setup_inputs(seed: int = 0) -> dict:
    key = jax.random.key(seed)
    ks = jax.random.split(key, 18)
    return {
        "x_prompt": _normal(ks[0], (BATCH, SEQ, D_MODEL), 1.0),
        "x_sample": _normal(ks[1], (DEC_BATCH, DEC_SEQ, D_MODEL), 1.0),
        "c_prompt": _normal(ks[2], (BATCH, D_MODEL), 1.0),
        "c_sample": _normal(ks[3], (DEC_BATCH, D_MODEL), 1.0),
        "w_ada": _normal(ks[4], (DEPTH, D_MODEL, 3 * D_MODEL), D_MODEL ** -0.5),
        "b_ada": _normal(ks[5], (DEPTH, 3 * D_MODEL), 0.02),
        "w_in": _normal(ks[6], (DEPTH, D_MODEL, N_IN), D_MODEL ** -0.5),
        "a_q_norm": 1.0 + _normal(ks[7], (DEPTH, A_Q_LORA), 0.1),
        "a_q_up": _normal(ks[8], (DEPTH, A_Q_LORA, A_HEADS * (A_NOPE + A_ROPE)), A_Q_LORA ** -0.5),
        "a_kv_norm": 1.0 + _normal(ks[9], (DEPTH, A_KV_LORA), 0.1),
        "a_kv_up": _normal(ks[10], (DEPTH, A_KV_LORA, A_HEADS * (A_NOPE + HEAD_DIM)), A_KV_LORA ** -0.5),
        "b_sink": _normal(ks[11], (DEPTH, B_HEADS), 0.5),
        "d_q_norm": 1.0 + _normal(ks[12], (DEPTH, HEAD_DIM), 0.1),
        "d_k_norm": 1.0 + _normal(ks[13], (DEPTH, HEAD_DIM), 0.1),
        "w_out": _normal(ks[14], (DEPTH, D_MIX, D_MODEL), BETA * D_MIX ** -0.5),
        "ln_g": 1.0 + _normal(ks[15], (DEPTH, D_MODEL), 0.1),
        "ln_b": _normal(ks[16], (DEPTH, D_MODEL), 0.02),
        "rel_bias": _normal(ks[17], (NUM_BUCKETS, B_HEADS + C_HEADS), 0.5),
    }


def reference(x_prompt, x_sample, c_prompt, c_sample, w_ada, b_ada, w_in, a_q_norm, a_q_up, a_kv_norm,
              a_kv_up, b_sink, d_q_norm, d_k_norm, w_out, ln_g, ln_b, rel_bias):
    y_prompt = trunk(x_prompt, c_prompt, w_ada, b_ada, w_in, a_q_norm, a_q_up, a_kv_norm, a_kv_up, b_sink,
                     d_q_norm, d_k_norm, w_out, ln_g, ln_b, rel_bias)
    y_sample = trunk(x_sample, c_sample, w_ada, b_ada, w_in, a_q_norm, a_q_up, a_kv_norm, a_kv_up, b_sink,
                     d_q_norm, d_k_norm, w_out, ln_g, ln_b, rel_bias)
    return (y_prompt, y_sample)
```

```python
import functools
import math

import numpy as np
import jax
import jax.numpy as jnp
from jax import lax
from jax.experimental import pallas as pl
from jax.experimental.pallas import tpu as pltpu

F32 = jnp.float32
BF16 = jnp.bfloat16

D_MODEL = 2048
DEPTH = 2
HEAD_DIM = 64
N_HEADS = 12
A_Q_LORA = 384
A_KV_LORA = 256
A_NOPE = 64
A_ROPE = 32
A_QK_PAD = 128
B_KV_HEADS = 4
B_WINDOW = 128
DILATED_PAIRS = ((128, 1), (512, 4), (2048, 16))
D_KV_HEADS = 4
GROUP_WIDTH = N_HEADS * HEAD_DIM
D_MIX = 4 * GROUP_WIDTH
A_COLS = A_Q_LORA + A_KV_LORA + A_ROPE
B_COLS = (N_HEADS + 2 * B_KV_HEADS) * HEAD_DIM
C_COLS = 3 * N_HEADS * HEAD_DIM
D_COLS = (N_HEADS + 2 * D_KV_HEADS) * HEAD_DIM
GRID_W = 64
NUM_BUCKETS = 32
T5_MAX_DISTANCE = 1024
ROPE_THETA = 10000.0
EPS = 1e-6
NEG = -1e30
ALPHA = (2 * DEPTH) ** 0.25
LOG2E = 1.4426950408889634

Z_ROW = 0
A_ROW = D_MIX
A_ROWS = 768
B_ROW = A_ROW + A_ROWS
D_ROW = B_ROW + B_COLS
C_ROW = D_ROW + D_COLS
N_PROJ = C_ROW + C_COLS

LANE = 128
BAND_QB = 128
VMEM_LIMIT = 52 * 2 ** 20


def _cparams(sem):
    return pltpu.CompilerParams(dimension_semantics=sem, vmem_limit_bytes=VMEM_LIMIT)


def _silu(x):
    return x / (1.0 + jnp.exp(-x))


def _tn_dot(a, b):
    return lax.dot_general(a, b, (((0,), (0,)), ((), ())), preferred_element_type=F32)


def _nt_dot(a, b):
    return lax.dot_general(a, b, (((1,), (1,)), ((), ())), preferred_element_type=F32)


def _mod_kernel(c_ref, w_ref, b_ref, o_ref):
    sc = _silu(c_ref[...]).astype(BF16)
    o_ref[0] = jnp.dot(sc, w_ref[0].astype(BF16), preferred_element_type=F32) + b_ref[0]


def _modulation(c_all, w_ada, b_ada):
    depth, _, n = w_ada.shape
    rows = c_all.shape[0]
    tn = 768
    return pl.pallas_call(
        _mod_kernel,
        grid=(depth, n // tn),
        in_specs=[pl.BlockSpec((rows, D_MODEL), lambda l, j: (0, 0)),
                  pl.BlockSpec((1, D_MODEL, tn), lambda l, j: (l, 0, j)),
                  pl.BlockSpec((1, 1, tn), lambda l, j: (l, 0, j))],
        out_specs=pl.BlockSpec((1, rows, tn), lambda l, j: (l, 0, j)),
        out_shape=jax.ShapeDtypeStruct((depth, rows, n), F32),
        compiler_params=_cparams(("parallel", "parallel")),
        name="adaln_modulation",
    )(c_all, w_ada, b_ada.reshape(depth, 1, n))


def _inproj_kernel(x_ref, sh_ref, sc_ref, w_ref, o_ref, h_ref):
    @pl.when(pl.program_id(2) == 0)
    def _():
        x = x_ref[0]
        mu = jnp.mean(x, axis=-1, keepdims=True)
        xc = x - mu
        var = jnp.mean(xc * xc, axis=-1, keepdims=True)
        h = (xc * lax.rsqrt(var + EPS)) * (1.0 + sc_ref[0]) + sh_ref[0]
        h_ref[...] = h.astype(BF16)

    o_ref[0] = _nt_dot(w_ref[...], h_ref[...]).astype(BF16)


def _input_projection(x, mod, w_t, tm, tn):
    b, s, _ = x.shape
    n = w_t.shape[0]
    return pl.pallas_call(
        _inproj_kernel,
        grid=(b, s // tm, n // tn),
        in_specs=[pl.BlockSpec((1, tm, D_MODEL), lambda i, t, j: (i, t, 0)),
                  pl.BlockSpec((1, 1, D_MODEL), lambda i, t, j: (i, 0, 0)),
                  pl.BlockSpec((1, 1, D_MODEL), lambda i, t, j: (i, 0, 1)),
                  pl.BlockSpec((tn, D_MODEL), lambda i, t, j: (j, 0))],
        out_specs=pl.BlockSpec((1, tn, tm), lambda i, t, j: (i, j, t)),
        out_shape=jax.ShapeDtypeStruct((b, n, s), BF16),
        scratch_shapes=[pltpu.VMEM((tm, D_MODEL), BF16)],
        compiler_params=_cparams(("parallel", "parallel", "arbitrary")),
        name="input_projection",
    )(x, mod, mod, w_t)


def _rms_rows(x, g):
    return x * lax.rsqrt(jnp.mean(x * x, axis=0, keepdims=True) + EPS) * g


def _mla_prep_kernel(a_ref, gq_ref, gkv_ref, wq_ref, wk_ref, wv_ref, rc_ref, rs_ref,
                     q_ref, k_ref, v_ref, *, q_scale):
    a = a_ref[0].astype(F32)
    tm = a.shape[1]
    qn = _rms_rows(a[0:A_Q_LORA], gq_ref[...]).astype(BF16)
    kvn = _rms_rows(a[A_Q_LORA:A_Q_LORA + A_KV_LORA], gkv_ref[...])
    kpe = a[A_Q_LORA + A_KV_LORA:A_COLS]
    rc = rc_ref[...]
    rs = rs_ref[...]
    half = A_ROPE // 2
    lo, mid, hi = A_NOPE, A_NOPE + half, A_NOPE + A_ROPE

    q = jnp.dot(wq_ref[...], qn, preferred_element_type=F32)
    q = q.reshape(N_HEADS, A_QK_PAD, tm)
    q_sw = jnp.concatenate([q[:, :lo], q[:, mid:hi], q[:, lo:mid], q[:, hi:]], axis=1)
    q = (q * rc[None] + q_sw * rs[None]) * q_scale
    q_ref[0] = q.reshape(N_HEADS * A_QK_PAD, tm).astype(BF16)

    kpe_sw = jnp.concatenate([kpe[half:], kpe[:half]], axis=0)
    kpe = kpe * rc[lo:hi] + kpe_sw * rs[lo:hi]
    xk = jnp.concatenate([kvn, kpe], axis=0).astype(BF16)
    k_ref[0] = jnp.dot(wk_ref[...], xk, preferred_element_type=F32).astype(BF16)
    v_ref[0] = jnp.dot(wv_ref[...], xk[:A_KV_LORA], preferred_element_type=F32).astype(BF16)


def _mla_prep(proj, gq, gkv, wq_t, wk_t, wv_t, rope_c, rope_s, tm):
    b, _, s = proj.shape
    hq = N_HEADS * A_QK_PAD
    full = lambda shape: pl.BlockSpec(shape, lambda i, t: (0,) * len(shape))
    return pl.pallas_call(
        functools.partial(_mla_prep_kernel, q_scale=(A_NOPE + A_ROPE) ** -0.5 * LOG2E),
        grid=(b, s // tm),
        in_specs=[pl.BlockSpec((1, A_ROWS, tm), lambda i, t: (i, A_ROW // A_ROWS, t)),
                  full((A_Q_LORA, 1)), full((A_KV_LORA, 1)),
                  full(wq_t.shape), full(wk_t.shape), full(wv_t.shape),
                  pl.BlockSpec((A_QK_PAD, tm), lambda i, t: (0, t)),
                  pl.BlockSpec((A_QK_PAD, tm), lambda i, t: (0, t))],
        out_specs=[pl.BlockSpec((1, hq, tm), lambda i, t: (i, 0, t)),
                   pl.BlockSpec((1, hq, tm), lambda i, t: (i, 0, t)),
                   pl.BlockSpec((1, GROUP_WIDTH, tm), lambda i, t: (i, 0, t))],
        out_shape=[jax.ShapeDtypeStruct((b, hq, s), BF16),
                   jax.ShapeDtypeStruct((b, hq, s), BF16),
                   jax.ShapeDtypeStruct((b, GROUP_WIDTH, s), BF16)],
        compiler_params=_cparams(("parallel", "parallel")),
        name="latent_attention_prep",
    )(proj, gq, gkv, wq_t, wk_t, wv_t, rope_c, rope_s)


def _axial_prep_kernel(x_ref, g_ref, rc_ref, rs_ref, q_ref, k_ref, *, q_scale):
    x = x_ref[0].astype(F32)
    tm = x.shape[1]
    nh = N_HEADS + D_KV_HEADS
    x = x.reshape(nh, HEAD_DIM, tm)
    x = x * lax.rsqrt(jnp.mean(x * x, axis=1, keepdims=True) + EPS) * g_ref[...]
    q4 = HEAD_DIM // 4
    x_sw = jnp.concatenate([x[:, q4:2 * q4], x[:, :q4], x[:, 3 * q4:], x[:, 2 * q4:3 * q4]], axis=1)
    x = x * rc_ref[...][None] + x_sw * rs_ref[...][None]
    q_ref[0] = (x[:N_HEADS] * q_scale).reshape(GROUP_WIDTH, tm).astype(BF16)
    k_ref[0] = x[N_HEADS:].reshape(D_KV_HEADS * HEAD_DIM, tm).astype(BF16)


def _axial_prep(proj, gains, rope_c, rope_s, tm):
    b, _, s = proj.shape
    rows = GROUP_WIDTH + D_KV_HEADS * HEAD_DIM
    return pl.pallas_call(
        functools.partial(_axial_prep_kernel, q_scale=HEAD_DIM ** -0.5 * LOG2E),
        grid=(b, s // tm),
        in_specs=[pl.BlockSpec((1, rows, tm), lambda i, t: (i, D_ROW // rows, t)),
                  pl.BlockSpec((N_HEADS + D_KV_HEADS, HEAD_DIM, 1), lambda i, t: (0, 0, 0)),
                  pl.BlockSpec((HEAD_DIM, tm), lambda i, t: (0, t)),
                  pl.BlockSpec((HEAD_DIM, tm), lambda i, t: (0, t))],
        out_specs=[pl.BlockSpec((1, GROUP_WIDTH, tm), lambda i, t: (i, 0, t)),
                   pl.BlockSpec((1, D_KV_HEADS * HEAD_DIM, tm), lambda i, t: (i, 0, t))],
        out_shape=[jax.ShapeDtypeStruct((b, GROUP_WIDTH, s), BF16),
                   jax.ShapeDtypeStruct((b, D_KV_HEADS * HEAD_DIM, s), BF16)],
        compiler_params=_cparams(("parallel", "parallel")),
        name="axial_attention_prep",
    )(proj, gains, rope_c, rope_s)


def _flash_kernel(q_ref, k_ref, v_ref, o_ref, *, groups, dq, tk):
    tq = q_ref.shape[2]
    nk = k_ref.shape[2] // tk
    for g in range(groups):
        q_t = q_ref[0, g * dq:(g + 1) * dq, :]

        def body(c, carry, q_t=q_t):
            m, l, acc = carry
            off = pl.multiple_of(c * tk, tk)
            k_t = k_ref[0, :, pl.ds(off, tk)]
            v_t = v_ref[0, :, pl.ds(off, tk)]
            s = _tn_dot(k_t, q_t)
            m_new = jnp.maximum(m, jnp.max(s, axis=0, keepdims=True))
            alpha = jnp.exp2(m - m_new)
            p = jnp.exp2(s - m_new)
            l = alpha * l + jnp.sum(p, axis=0, keepdims=True)
            acc = alpha * acc + jnp.dot(v_t, p.astype(BF16), preferred_element_type=F32)
            return m_new, l, acc

        init = (jnp.full((1, tq), NEG, F32), jnp.zeros((1, tq), F32), jnp.zeros((HEAD_DIM, tq), F32))
        _, l, acc = lax.fori_loop(0, nk, body, init)
        o_ref[0, g * HEAD_DIM:(g + 1) * HEAD_DIM, :] = (acc / l).astype(BF16)


def _dense_attention(q, k, v, *, kv_heads, dq, k_blk0, v_blk0, tq, tk):
    b, _, s = q.shape
    groups = N_HEADS // kv_heads
    return pl.pallas_call(
        functools.partial(_flash_kernel, groups=groups, dq=dq, tk=tk),
        grid=(b, kv_heads, s // tq),
        in_specs=[pl.BlockSpec((1, groups * dq, tq), lambda i, h, t: (i, h, t)),
                  pl.BlockSpec((1, dq, s), lambda i, h, t: (i, k_blk0 + h, 0)),
                  pl.BlockSpec((1, HEAD_DIM, s), lambda i, h, t: (i, v_blk0 + h, 0))],
        out_specs=pl.BlockSpec((1, groups * HEAD_DIM, tq), lambda i, h, t: (i, h, t)),
        out_shape=jax.ShapeDtypeStruct((b, GROUP_WIDTH, s), BF16),
        compiler_params=_cparams(("parallel", "parallel", "arbitrary")),
        name="dense_attention",
    )(q, k, v)


def _t5_bucket(rel):
    half = NUM_BUCKETS // 2
    exact = half // 2
    n = jnp.abs(rel)
    large = exact + (jnp.log(jnp.maximum(n, 1).astype(F32) / exact)
                     / math.log(T5_MAX_DISTANCE / exact) * (half - exact)).astype(jnp.int32)
    large = jnp.minimum(large, half - 1)
    return jnp.where(rel > 0, half, 0) + jnp.where(n < exact, n, large)


def _band_buckets(w, dilation, kw):
    kk = jnp.arange(kw)[:, None]
    qq = jnp.arange(BAND_QB)[None, :]
    out = []
    for shift in (0, BAND_QB, kw - BAND_QB):
        rel = kk - shift - qq
        out.append(jnp.where(jnp.abs(rel) <= w, _t5_bucket(dilation * rel), -1))
    return jnp.stack(out).astype(jnp.int32)


def _bias_kernel(bucket_ref, tab_ref, o_ref, *, groups):
    h = pl.program_id(0)
    bk = bucket_ref[...]
    for g in range(groups):
        acc = jnp.zeros(bk.shape, F32)
        for b in range(NUM_BUCKETS):
            acc = jnp.where(bk == b, tab_ref[b, h * groups + g], acc)
        o_ref[:, 0, :, g * BAND_QB:(g + 1) * BAND_QB] = jnp.where(bk < 0, NEG, acc * LOG2E)


def _band_bias(table, w, dilation, kw, kv_heads):
    groups = N_HEADS // kv_heads
    buckets = _band_buckets(w, dilation, kw)
    return pl.pallas_call(
        functools.partial(_bias_kernel, groups=groups),
        grid=(kv_heads,),
        in_specs=[pl.BlockSpec((3, kw, BAND_QB), lambda h: (0, 0, 0)),
                  pl.BlockSpec(memory_space=pltpu.SMEM)],
        out_specs=pl.BlockSpec((3, 1, kw, groups * BAND_QB), lambda h: (0, h, 0, 0)),
        out_shape=jax.ShapeDtypeStruct((3, kv_heads, kw, groups * BAND_QB), F32),
        compiler_params=_cparams(("parallel",)),
        name="relative_bias_tables",
    )(buckets, table)


def _banded_kernel(q_ref, k_ref, v_ref, tab_ref, *rest, groups, heads, kw, has_sink):
    if has_sink:
        sink_ref, o_ref, lse_ref = rest
    else:
        o_ref, lse_ref = rest
    tq = q_ref.shape[2]
    seq = k_ref.shape[2]
    t0 = pl.program_id(2) * tq
    qb = BAND_QB

    def sub_block(j, carry):
        qoff = pl.multiple_of(j * qb, qb)
        q0 = t0 + qoff
        sel = jnp.where(q0 == 0, 0, jnp.where(q0 == seq - qb, 2, 1))
        ks = pl.multiple_of(jnp.clip(q0 - qb, 0, seq - kw), qb)
        for hb in range(heads):
            k_t = k_ref[0, hb * HEAD_DIM:(hb + 1) * HEAD_DIM, pl.ds(ks, kw)]
            v_t = v_ref[0, hb * HEAD_DIM:(hb + 1) * HEAD_DIM, pl.ds(ks, kw)]
            r0 = hb * groups * HEAD_DIM
            q_cat = jnp.concatenate(
                [q_ref[0, r0 + g * HEAD_DIM:r0 + (g + 1) * HEAD_DIM, pl.ds(qoff, qb)]
                 for g in range(groups)], axis=1)
            s = _tn_dot(k_t, q_cat) * (HEAD_DIM ** -0.5 * LOG2E) + tab_ref[sel, hb]
            m = jnp.max(s, axis=0, keepdims=True)
            if has_sink:
                sink = sink_ref[hb]
                m = jnp.maximum(m, sink)
            p = jnp.exp2(s - m)
            l = jnp.sum(p, axis=0, keepdims=True)
            if has_sink:
                l = l + jnp.exp2(sink - m)
            o = jnp.dot(v_t, p.astype(BF16), preferred_element_type=F32) / l
            lse = m + jnp.log2(l)
            for g in range(groups):
                o_ref[0, r0 + g * HEAD_DIM:r0 + (g + 1) * HEAD_DIM, pl.ds(qoff, qb)] = (
                    o[:, g * qb:(g + 1) * qb].astype(o_ref.dtype))
                lse_ref[0, hb, g:g + 1, pl.ds(qoff, qb)] = lse[:, g * qb:(g + 1) * qb]
        return carry

    lax.fori_loop(0, tq // qb, sub_block, 0)


def _banded_attention(q, k, v, bias, sink, *, kv_heads, heads_per_step, q_row, k_row, v_row, tq):
    b, _, seq = q.shape
    groups = N_HEADS // kv_heads
    hb = heads_per_step
    kw = bias.shape[2]
    qrows, krows = hb * groups * HEAD_DIM, hb * HEAD_DIM
    in_specs = [pl.BlockSpec((1, qrows, tq), lambda i, h, t: (i, q_row // qrows + h, t)),
                pl.BlockSpec((1, krows, seq), lambda i, h, t: (i, k_row // krows + h, 0)),
                pl.BlockSpec((1, krows, seq), lambda i, h, t: (i, v_row // krows + h, 0)),
                pl.BlockSpec((3, hb, kw, groups * BAND_QB), lambda i, h, t: (0, h, 0, 0))]
    args = [q, k, v, bias]
    if sink is not None:
        in_specs.append(pl.BlockSpec((hb, 1, groups * BAND_QB), lambda i, h, t: (h, 0, 0)))
        args.append(sink)
    return pl.pallas_call(
        functools.partial(_banded_kernel, groups=groups, heads=hb, kw=kw, has_sink=sink is not None),
        grid=(b, kv_heads // hb, seq // tq),
        in_specs=in_specs,
        out_specs=[pl.BlockSpec((1, qrows, tq), lambda i, h, t: (i, h, t)),
                   pl.BlockSpec((1, hb, groups, tq), lambda i, h, t: (i, h, 0, t))],
        out_shape=[jax.ShapeDtypeStruct((b, GROUP_WIDTH, seq), BF16),
                   jax.ShapeDtypeStruct((b, kv_heads, groups, seq), F32)],
        compiler_params=_cparams(("parallel", "parallel", "arbitrary")),
        name="banded_attention",
    )(*args)


def _mix_kernel(o1_ref, o2_ref, o3_ref, l1_ref, l2_ref, l3_ref, y_ref):
    for h in range(N_HEADS):
        a1, a2, a3 = l1_ref[0, h], l2_ref[0, h], l3_ref[0, h]
        mx = jnp.maximum(jnp.maximum(a1, a2), a3)
        e1, e2, e3 = jnp.exp2(a1 - mx), jnp.exp2(a2 - mx), jnp.exp2(a3 - mx)
        inv = 1.0 / (e1 + e2 + e3)
        rows = slice(h * HEAD_DIM, (h + 1) * HEAD_DIM)
        y = ((e1 * inv) * o1_ref[0, rows, :].astype(F32) + (e2 * inv) * o2_ref[0, rows, :].astype(F32)
             + (e3 * inv) * o3_ref[0, rows, :].astype(F32))
        y_ref[0, rows, :] = y.astype(BF16)


def _dilated_mix(outs, lses, tm):
    b, _, s = outs[0].shape
    ospec = pl.BlockSpec((1, GROUP_WIDTH, tm), lambda i, t: (i, 0, t))
    lspec = pl.BlockSpec((1, N_HEADS, 1, tm), lambda i, t: (i, 0, 0, t))
    return pl.pallas_call(
        _mix_kernel,
        grid=(b, s // tm),
        in_specs=[ospec] * 3 + [lspec] * 3,
        out_specs=ospec,
        out_shape=jax.ShapeDtypeStruct((b, GROUP_WIDTH, s), BF16),
        compiler_params=_cparams(("parallel", "parallel")),
        name="dilated_mixture",
    )(*outs, *lses)


def _outproj_kernel(ya_ref, yb_ref, yc_ref, yd_ref, z_ref, x_ref, gate_ref, w_ref, g_ref, b_ref, o_ref):
    acc = None
    for i, y_ref in enumerate((ya_ref, yb_ref, yc_ref, yd_ref)):
        rows = slice(i * GROUP_WIDTH, (i + 1) * GROUP_WIDTH)
        z = z_ref[0, rows, :].astype(F32)
        gated = (y_ref[0].astype(F32) * _silu(z)).astype(BF16)
        part = _tn_dot(gated, w_ref[rows, :])
        acc = part if acc is None else acc + part
    r = ALPHA * x_ref[0] + gate_ref[0] * acc
    mu = jnp.mean(r, axis=-1, keepdims=True)
    rc = r - mu
    var = jnp.mean(rc * rc, axis=-1, keepdims=True)
    o_ref[0] = (rc * lax.rsqrt(var + EPS)) * g_ref[...] + b_ref[...]


def _output_projection(ys, proj, x, mod, w_out, ln_g, ln_b, tm):
    b, s, _ = x.shape
    yspec = pl.BlockSpec((1, GROUP_WIDTH, tm), lambda i, t: (i, 0, t))
    return pl.pallas_call(
        _outproj_kernel,
        grid=(b, s // tm),
        in_specs=[yspec] * 4 + [
            pl.BlockSpec((1, D_MIX, tm), lambda i, t: (i, Z_ROW // D_MIX, t)),
            pl.BlockSpec((1, tm, D_MODEL), lambda i, t: (i, t, 0)),
            pl.BlockSpec((1, 1, D_MODEL), lambda i, t: (i, 0, 2)),
            pl.BlockSpec((D_MIX, D_MODEL), lambda i, t: (0, 0)),
            pl.BlockSpec((1, D_MODEL), lambda i, t: (0, 0)),
            pl.BlockSpec((1, D_MODEL), lambda i, t: (0, 0))],
        out_specs=pl.BlockSpec((1, tm, D_MODEL), lambda i, t: (i, t, 0)),
        out_shape=jax.ShapeDtypeStruct((b, s, D_MODEL), F32),
        compiler_params=_cparams(("parallel", "parallel")),
        name="output_projection",
    )(*ys, proj, x, mod, w_out, ln_g, ln_b)


def _rope_tables(seq):
    pos = jnp.arange(seq)
    inv = ROPE_THETA ** (-jnp.arange(0, A_ROPE, 2, dtype=F32) / A_ROPE)

    def cs(p):
        ang = p.astype(F32)[:, None] * inv[None, :]
        return jnp.cos(ang).T, jnp.sin(ang).T

    c, sn = cs(pos)
    ones, zeros = jnp.ones((A_NOPE, seq), F32), jnp.zeros((A_NOPE, seq), F32)
    pad = jnp.zeros((A_QK_PAD - A_NOPE - A_ROPE, seq), F32)
    a_c = jnp.concatenate([ones, c, c, pad], axis=0)
    a_s = jnp.concatenate([zeros, -sn, sn, pad], axis=0)
    cr, sr = cs(pos // GRID_W)
    cc, sc = cs(pos % GRID_W)
    d_c = jnp.concatenate([cr, cr, cc, cc], axis=0)
    d_s = jnp.concatenate([-sr, sr, -sc, sc], axis=0)
    return a_c, a_s, d_c, d_s


def _layer_weights(l, w_in, a_q_norm, a_q_up, a_kv_norm, a_kv_up, b_sink, d_q_norm, d_k_norm, w_out,
                   ln_g, ln_b):
    wa, wb, wc, wd, wz = jnp.split(w_in[l], np.cumsum([A_COLS, B_COLS, C_COLS, D_COLS]).tolist(), axis=1)
    wa = jnp.pad(wa, ((0, 0), (0, A_ROWS - A_COLS)))
    w_t = jnp.concatenate([wz, wa, wb, wd, wc], axis=1).T.astype(BF16)
    dqk = A_NOPE + A_ROPE
    wq = jnp.pad(a_q_up[l].reshape(A_Q_LORA, N_HEADS, dqk), ((0, 0), (0, 0), (0, A_QK_PAD - dqk)))
    wq_t = wq.reshape(A_Q_LORA, N_HEADS * A_QK_PAD).T.astype(BF16)
    kvu = a_kv_up[l].reshape(A_KV_LORA, N_HEADS, A_NOPE + HEAD_DIM)
    wk_top = jnp.pad(kvu[:, :, :A_NOPE], ((0, 0), (0, 0), (0, A_QK_PAD - A_NOPE)))
    sel = jnp.zeros((A_ROPE, N_HEADS, A_QK_PAD), F32)
    sel = sel.at[:, :, A_NOPE:A_NOPE + A_ROPE].set(jnp.eye(A_ROPE, dtype=F32)[:, None, :])
    wk_t = jnp.concatenate([wk_top, sel], axis=0).reshape(A_KV_LORA + A_ROPE, -1).T.astype(BF16)
    wv_t = kvu[:, :, A_NOPE:].reshape(A_KV_LORA, GROUP_WIDTH).T.astype(BF16)
    d_gain = jnp.concatenate([jnp.broadcast_to(d_q_norm[l], (N_HEADS, HEAD_DIM)),
                              jnp.broadcast_to(d_k_norm[l], (D_KV_HEADS, HEAD_DIM))])[:, :, None]
    groups = N_HEADS // B_KV_HEADS
    sink = jnp.repeat(b_sink[l].reshape(B_KV_HEADS, 1, groups) * LOG2E, BAND_QB, axis=2)
    return dict(w_t=w_t, gq=a_q_norm[l][:, None], gkv=a_kv_norm[l][:, None], wq_t=wq_t, wk_t=wk_t,
                wv_t=wv_t, d_gain=d_gain, sink=sink, w_out=w_out[l].astype(BF16),
                ln_g=ln_g[l][None, :], ln_b=ln_b[l][None, :])


def _to_strided(t, r):
    b, f, s = t.shape
    return t.reshape(b, f, s // r, r).transpose(0, 3, 1, 2).reshape(b * r, f, s // r)


def _from_strided(t, b, r):
    rest = t.shape[1:-1]
    sr = t.shape[-1]
    t = jnp.moveaxis(t.reshape((b, r) + rest + (sr,)), 1, -1)
    return t.reshape((b,) + rest + (sr * r,))


def _pick(n, pref):
    return min(n, pref)


def _layer(x, mod, wts, bias_b, bias_c, ropes):
    b, s, _ = x.shape
    a_c, a_s, d_c, d_s = ropes
    proj = _input_projection(x, mod, wts["w_t"], tm=_pick(s, 1024), tn=1088)

    tp = _pick(s, 512)
    q_a, k_a, v_a = _mla_prep(proj, wts["gq"], wts["gkv"], wts["wq_t"], wts["wk_t"], wts["wv_t"],
                              a_c, a_s, tp)
    y_a = _dense_attention(q_a, k_a, v_a, kv_heads=N_HEADS, dq=A_QK_PAD, k_blk0=0, v_blk0=0,
                           tq=_pick(s, 512), tk=_pick(s, 512))

    y_b, _ = _banded_attention(proj, proj, proj, bias_b, wts["sink"], kv_heads=B_KV_HEADS,
                               heads_per_step=1, q_row=B_ROW, k_row=B_ROW + GROUP_WIDTH,
                               v_row=B_ROW + GROUP_WIDTH + B_KV_HEADS * HEAD_DIM, tq=_pick(s, 1024))

    outs, lses = [], []
    for (window, r), bias in zip(DILATED_PAIRS, bias_c):
        if r == 1:
            o, lse = _banded_attention(proj, proj, proj, bias, None, kv_heads=N_HEADS, heads_per_step=2,
                                       q_row=C_ROW, k_row=C_ROW + GROUP_WIDTH,
                                       v_row=C_ROW + 2 * GROUP_WIDTH, tq=_pick(s, 1024))
        else:
            qkv = _to_strided(proj[:, C_ROW:, :], r)
            o, lse = _banded_attention(qkv, qkv, qkv, bias, None, kv_heads=N_HEADS, heads_per_step=4,
                                       q_row=0, k_row=GROUP_WIDTH, v_row=2 * GROUP_WIDTH,
                                       tq=_pick(s // r, 1024))
            o, lse = _from_strided(o, b, r), _from_strided(lse, b, r)
        outs.append(o)
        lses.append(lse)
    y_c = _dilated_mix(outs, lses, _pick(s, 1024))

    q_d, k_d = _axial_prep(proj, wts["d_gain"], d_c, d_s, tp)
    y_d = _dense_attention(q_d, k_d, proj, kv_heads=D_KV_HEADS, dq=HEAD_DIM, k_blk0=0,
                           v_blk0=(D_ROW + GROUP_WIDTH + D_KV_HEADS * HEAD_DIM) // HEAD_DIM,
                           tq=_pick(s, 512), tk=_pick(s, 512))

    return _output_projection((y_a, y_b, y_c, y_d), proj, x, mod, wts["w_out"], wts["ln_g"], wts["ln_b"],
                              tm=_pick(s, 256))


def _band_kw(seq):
    return min(seq, BAND_QB + 2 * LANE)


def _trunk(x, mods, layer_wts, rel_bias):
    b, s, _ = x.shape
    ropes = _rope_tables(s)
    bias_b = _band_bias(rel_bias[:, :N_HEADS], B_WINDOW, 1, _band_kw(s), B_KV_HEADS)
    bias_c = [_band_bias(rel_bias[:, N_HEADS:], window // (2 * r), r, _band_kw(s // r), N_HEADS)
              for window, r in DILATED_PAIRS]
    for l in range(DEPTH):
        x = _layer(x, mods[l], layer_wts[l], bias_b, bias_c, ropes)
    return x


def kernel(x_prompt, x_sample, c_prompt, c_sample, w_ada, b_ada, w_in, a_q_norm, a_q_up, a_kv_norm, a_kv_up, b_sink, d_q_norm, d_k_norm, w_out, ln_g, ln_b, rel_bias):
    bp, bs = c_prompt.shape[0], c_sample.shape[0]
    rows = -(-(bp + bs) // 16) * 16
    c_all = jnp.pad(jnp.concatenate([c_prompt, c_sample], axis=0), ((0, rows - bp - bs), (0, 0)))
    mod = _modulation(c_all, w_ada, b_ada)
    layer_wts = [_layer_weights(l, w_in, a_q_norm, a_q_up, a_kv_norm, a_kv_up, b_sink, d_q_norm,
                                d_k_norm, w_out, ln_g, ln_b) for l in range(DEPTH)]
    mods_p = [mod[l, :bp].reshape(bp, 1, -1) for l in range(DEPTH)]
    mods_s = [mod[l, bp:bp + bs].reshape(bs, 1, -1) for l in range(DEPTH)]
    y_prompt = _trunk(x_prompt, mods_p, layer_wts, rel_bias)
    y_sample = _trunk(x_sample, mods_s, layer_wts, rel_bias)
    return (y_prompt, y_sample)
```

```python
import functools
import math

import numpy as np
import jax
import jax.numpy as jnp
from jax import lax
from jax.experimental import pallas as pl
from jax.experimental.pallas import tpu as pltpu

F32 = jnp.float32
BF16 = jnp.bfloat16

D_MODEL = 2048
DEPTH = 2
HEAD_DIM = 64
N_HEADS = 12
A_Q_LORA = 384
A_KV_LORA = 256
A_NOPE = 64
A_ROPE = 32
A_QK_PAD = 128
B_KV_HEADS = 4
B_WINDOW = 128
DILATED_PAIRS = ((128, 1), (512, 4), (2048, 16))
D_KV_HEADS = 4
GROUP_WIDTH = N_HEADS * HEAD_DIM
D_MIX = 4 * GROUP_WIDTH
A_COLS = A_Q_LORA + A_KV_LORA + A_ROPE
B_COLS = (N_HEADS + 2 * B_KV_HEADS) * HEAD_DIM
C_COLS = 3 * N_HEADS * HEAD_DIM
D_COLS = (N_HEADS + 2 * D_KV_HEADS) * HEAD_DIM
GRID_W = 64
NUM_BUCKETS = 32
T5_MAX_DISTANCE = 1024
ROPE_THETA = 10000.0
EPS = 1e-6
NEG = -1e30
ALPHA = (2 * DEPTH) ** 0.25
LOG2E = 1.4426950408889634

Z_ROW = 0
A_ROW = D_MIX
A_ROWS = 768
B_ROW = A_ROW + A_ROWS
D_ROW = B_ROW + B_COLS
C_ROW = D_ROW + D_COLS
N_PROJ = C_ROW + C_COLS

LANE = 128
BAND_QB = 128
BAND_UNROLL = 4
BAND_AHEAD = 3
VMEM_LIMIT = 52 * 2 ** 20


def _cparams(sem):
    return pltpu.CompilerParams(dimension_semantics=sem, vmem_limit_bytes=VMEM_LIMIT)


def _silu(x):
    return x / (1.0 + jnp.exp(-x))


def _tn_dot(a, b):
    return lax.dot_general(a, b, (((0,), (0,)), ((), ())), preferred_element_type=F32)


def _nt_dot(a, b):
    return lax.dot_general(a, b, (((1,), (1,)), ((), ())), preferred_element_type=F32)


def _mod_kernel(c_ref, w_ref, b_ref, o_ref):
    sc = _silu(c_ref[...]).astype(BF16)
    o_ref[0] = jnp.dot(sc, w_ref[0].astype(BF16), preferred_element_type=F32) + b_ref[0]


def _modulation(c_all, w_ada, b_ada):
    depth, _, n = w_ada.shape
    rows = c_all.shape[0]
    tn = 768
    return pl.pallas_call(
        _mod_kernel,
        grid=(depth, n // tn),
        in_specs=[pl.BlockSpec((rows, D_MODEL), lambda l, j: (0, 0)),
                  pl.BlockSpec((1, D_MODEL, tn), lambda l, j: (l, 0, j)),
                  pl.BlockSpec((1, 1, tn), lambda l, j: (l, 0, j))],
        out_specs=pl.BlockSpec((1, rows, tn), lambda l, j: (l, 0, j)),
        out_shape=jax.ShapeDtypeStruct((depth, rows, n), F32),
        compiler_params=_cparams(("parallel", "parallel")),
        name="adaln_modulation",
    )(c_all, w_ada, b_ada.reshape(depth, 1, n))


def _inproj_kernel(x_ref, sh_ref, sc_ref, w_ref, o_ref, h_ref):
    @pl.when(pl.program_id(2) == 0)
    def _():
        x = x_ref[0]
        mu = jnp.mean(x, axis=-1, keepdims=True)
        xc = x - mu
        var = jnp.mean(xc * xc, axis=-1, keepdims=True)
        h = (xc * lax.rsqrt(var + EPS)) * (1.0 + sc_ref[0]) + sh_ref[0]
        h_ref[...] = h.astype(BF16)

    o_ref[0] = _nt_dot(w_ref[...], h_ref[...]).astype(BF16)


def _input_projection(x, mod, w_t, tm, tn):
    b, s, _ = x.shape
    n = w_t.shape[0]
    return pl.pallas_call(
        _inproj_kernel,
        grid=(b, s // tm, n // tn),
        in_specs=[pl.BlockSpec((1, tm, D_MODEL), lambda i, t, j: (i, t, 0)),
                  pl.BlockSpec((1, 1, D_MODEL), lambda i, t, j: (i, 0, 0)),
                  pl.BlockSpec((1, 1, D_MODEL), lambda i, t, j: (i, 0, 1)),
                  pl.BlockSpec((tn, D_MODEL), lambda i, t, j: (j, 0))],
        out_specs=pl.BlockSpec((1, tn, tm), lambda i, t, j: (i, j, t)),
        out_shape=jax.ShapeDtypeStruct((b, n, s), BF16),
        scratch_shapes=[pltpu.VMEM((tm, D_MODEL), BF16)],
        compiler_params=_cparams(("parallel", "parallel", "arbitrary")),
        name="input_projection",
    )(x, mod, mod, w_t)


def _rms_rows(x, g):
    return x * lax.rsqrt(jnp.mean(x * x, axis=0, keepdims=True) + EPS) * g


def _mla_prep_kernel(a_ref, gq_ref, gkv_ref, wq_ref, wk_ref, wv_ref, rc_ref, rs_ref,
                     q_ref, k_ref, v_ref, *, q_scale):
    a = a_ref[0].astype(F32)
    tm = a.shape[1]
    qn = _rms_rows(a[0:A_Q_LORA], gq_ref[...]).astype(BF16)
    kvn = _rms_rows(a[A_Q_LORA:A_Q_LORA + A_KV_LORA], gkv_ref[...])
    kpe = a[A_Q_LORA + A_KV_LORA:A_COLS]
    rc = rc_ref[...]
    rs = rs_ref[...]
    half = A_ROPE // 2
    lo, mid, hi = A_NOPE, A_NOPE + half, A_NOPE + A_ROPE

    q = jnp.dot(wq_ref[...], qn, preferred_element_type=F32)
    q = q.reshape(N_HEADS, A_QK_PAD, tm)
    q_sw = jnp.concatenate([q[:, :lo], q[:, mid:hi], q[:, lo:mid], q[:, hi:]], axis=1)
    q = (q * rc[None] + q_sw * rs[None]) * q_scale
    q_ref[0] = q.reshape(N_HEADS * A_QK_PAD, tm).astype(BF16)

    kpe_sw = jnp.concatenate([kpe[half:], kpe[:half]], axis=0)
    kpe = kpe * rc[lo:hi] + kpe_sw * rs[lo:hi]
    xk = jnp.concatenate([kvn, kpe], axis=0).astype(BF16)
    k_ref[0] = jnp.dot(wk_ref[...], xk, preferred_element_type=F32).astype(BF16)
    v_ref[0] = jnp.dot(wv_ref[...], xk[:A_KV_LORA], preferred_element_type=F32).astype(BF16)


def _mla_prep(proj, gq, gkv, wq_t, wk_t, wv_t, rope_c, rope_s, tm):
    b, _, s = proj.shape
    hq = N_HEADS * A_QK_PAD
    full = lambda shape: pl.BlockSpec(shape, lambda i, t: (0,) * len(shape))
    return pl.pallas_call(
        functools.partial(_mla_prep_kernel, q_scale=(A_NOPE + A_ROPE) ** -0.5 * LOG2E),
        grid=(b, s // tm),
        in_specs=[pl.BlockSpec((1, A_ROWS, tm), lambda i, t: (i, A_ROW // A_ROWS, t)),
                  full((A_Q_LORA, 1)), full((A_KV_LORA, 1)),
                  full(wq_t.shape), full(wk_t.shape), full(wv_t.shape),
                  pl.BlockSpec((A_QK_PAD, tm), lambda i, t: (0, t)),
                  pl.BlockSpec((A_QK_PAD, tm), lambda i, t: (0, t))],
        out_specs=[pl.BlockSpec((1, hq, tm), lambda i, t: (i, 0, t)),
                   pl.BlockSpec((1, hq, tm), lambda i, t: (i, 0, t)),
                   pl.BlockSpec((1, GROUP_WIDTH, tm), lambda i, t: (i, 0, t))],
        out_shape=[jax.ShapeDtypeStruct((b, hq, s), BF16),
                   jax.ShapeDtypeStruct((b, hq, s), BF16),
                   jax.ShapeDtypeStruct((b, GROUP_WIDTH, s), BF16)],
        compiler_params=_cparams(("parallel", "parallel")),
        name="latent_attention_prep",
    )(proj, gq, gkv, wq_t, wk_t, wv_t, rope_c, rope_s)


def _axial_prep_kernel(x_ref, g_ref, rc_ref, rs_ref, q_ref, k_ref, *, q_scale):
    x = x_ref[0].astype(F32)
    tm = x.shape[1]
    nh = N_HEADS + D_KV_HEADS
    x = x.reshape(nh, HEAD_DIM, tm)
    x = x * lax.rsqrt(jnp.mean(x * x, axis=1, keepdims=True) + EPS) * g_ref[...]
    q4 = HEAD_DIM // 4
    x_sw = jnp.concatenate([x[:, q4:2 * q4], x[:, :q4], x[:, 3 * q4:], x[:, 2 * q4:3 * q4]], axis=1)
    x = x * rc_ref[...][None] + x_sw * rs_ref[...][None]
    q_ref[0] = (x[:N_HEADS] * q_scale).reshape(GROUP_WIDTH, tm).astype(BF16)
    k_ref[0] = x[N_HEADS:].reshape(D_KV_HEADS * HEAD_DIM, tm).astype(BF16)


def _axial_prep(proj, gains, rope_c, rope_s, tm):
    b, _, s = proj.shape
    rows = GROUP_WIDTH + D_KV_HEADS * HEAD_DIM
    return pl.pallas_call(
        functools.partial(_axial_prep_kernel, q_scale=HEAD_DIM ** -0.5 * LOG2E),
        grid=(b, s // tm),
        in_specs=[pl.BlockSpec((1, rows, tm), lambda i, t: (i, D_ROW // rows, t)),
                  pl.BlockSpec((N_HEADS + D_KV_HEADS, HEAD_DIM, 1), lambda i, t: (0, 0, 0)),
                  pl.BlockSpec((HEAD_DIM, tm), lambda i, t: (0, t)),
                  pl.BlockSpec((HEAD_DIM, tm), lambda i, t: (0, t))],
        out_specs=[pl.BlockSpec((1, GROUP_WIDTH, tm), lambda i, t: (i, 0, t)),
                   pl.BlockSpec((1, D_KV_HEADS * HEAD_DIM, tm), lambda i, t: (i, 0, t))],
        out_shape=[jax.ShapeDtypeStruct((b, GROUP_WIDTH, s), BF16),
                   jax.ShapeDtypeStruct((b, D_KV_HEADS * HEAD_DIM, s), BF16)],
        compiler_params=_cparams(("parallel", "parallel")),
        name="axial_attention_prep",
    )(proj, gains, rope_c, rope_s)


V_EXT = HEAD_DIM + 16


def _flash_kernel(q_ref, k_ref, v_ref, o_ref, s_ref, *, groups, nsub, dq, tk):
    tq = q_ref.shape[2] // nsub
    nk = k_ref.shape[2] // tk
    chains = [(g, u) for g in range(groups) for u in range(nsub)]
    ones = jnp.ones((V_EXT - HEAD_DIM, tk), BF16)

    def scores(c, ci, slot):
        g, u = chains[ci]
        off = pl.multiple_of(c * tk, tk)
        s = _tn_dot(k_ref[0, :, pl.ds(off, tk)], q_ref[0, g * dq:(g + 1) * dq, u * tq:(u + 1) * tq])
        s_ref[ci, slot] = s
        return jnp.max(s, axis=0, keepdims=True)

    def consume(c, ci, slot, cmax, m, acc):
        off = pl.multiple_of(c * tk, tk)
        v_ext = jnp.concatenate([v_ref[0, :, pl.ds(off, tk)], ones], axis=0)
        m_new = jnp.maximum(m, cmax)
        alpha = jnp.exp2(m - m_new)
        p = jnp.exp2(s_ref[ci, slot] - m_new).astype(BF16)
        acc = alpha * acc + jnp.dot(v_ext, p, preferred_element_type=F32)
        return m_new, acc

    def pair(i, carry):
        c0 = 2 * i
        out = []
        for ci, (cmax, m, acc) in enumerate(carry):
            cmax1 = scores(c0 + 1, ci, 1)
            m, acc = consume(c0, ci, 0, cmax, m, acc)
            cmax0 = scores(c0 + 2, ci, 0)
            m, acc = consume(c0 + 1, ci, 1, cmax1, m, acc)
            out.append((cmax0, m, acc))
        return tuple(out)

    init = tuple((scores(0, ci, 0), jnp.full((1, tq), NEG, F32), jnp.zeros((V_EXT, tq), F32))
                 for ci in range(len(chains)))
    carry = lax.fori_loop(0, nk // 2 - 1, pair, init)
    for ci, (cmax, m, acc) in enumerate(carry):
        g, u = chains[ci]
        cmax1 = scores(nk - 1, ci, 1)
        m, acc = consume(nk - 2, ci, 0, cmax, m, acc)
        m, acc = consume(nk - 1, ci, 1, cmax1, m, acc)
        o_ref[0, g * HEAD_DIM:(g + 1) * HEAD_DIM, u * tq:(u + 1) * tq] = (
            acc[:HEAD_DIM] / acc[HEAD_DIM:HEAD_DIM + 1]).astype(BF16)


def _dense_attention(q, k, v, *, kv_heads, dq, k_blk0, v_blk0, tq, nsub, tk):
    b, _, s = q.shape
    groups = N_HEADS // kv_heads
    assert (s // tk) % 2 == 0 and s % (tq * nsub) == 0
    return pl.pallas_call(
        functools.partial(_flash_kernel, groups=groups, nsub=nsub, dq=dq, tk=tk),
        grid=(b, kv_heads, s // (tq * nsub)),
        in_specs=[pl.BlockSpec((1, groups * dq, tq * nsub), lambda i, h, t: (i, h, t)),
                  pl.BlockSpec((1, dq, s), lambda i, h, t: (i, k_blk0 + h, 0)),
                  pl.BlockSpec((1, HEAD_DIM, s), lambda i, h, t: (i, v_blk0 + h, 0))],
        out_specs=pl.BlockSpec((1, groups * HEAD_DIM, tq * nsub), lambda i, h, t: (i, h, t)),
        out_shape=jax.ShapeDtypeStruct((b, GROUP_WIDTH, s), BF16),
        scratch_shapes=[pltpu.VMEM((groups * nsub, 2, tk, tq), F32)],
        compiler_params=_cparams(("parallel", "parallel", "arbitrary")),
        name="dense_attention",
    )(q, k, v)


def _t5_bucket(rel):
    half = NUM_BUCKETS // 2
    exact = half // 2
    n = jnp.abs(rel)
    large = exact + (jnp.log(jnp.maximum(n, 1).astype(F32) / exact)
                     / math.log(T5_MAX_DISTANCE / exact) * (half - exact)).astype(jnp.int32)
    large = jnp.minimum(large, half - 1)
    return jnp.where(rel > 0, half, 0) + jnp.where(n < exact, n, large)


def _band_buckets(w, dilation, kw):
    kk = jnp.arange(kw)[:, None]
    qq = jnp.arange(BAND_QB)[None, :]
    out = []
    for shift in (0, BAND_QB, kw - BAND_QB):
        rel = kk - shift - qq
        out.append(jnp.where(jnp.abs(rel) <= w, _t5_bucket(dilation * rel), -1))
    return jnp.stack(out).astype(jnp.int32)


def _bias_kernel(bucket_ref, tab_ref, o_ref, *, groups):
    h = pl.program_id(0)
    bk = bucket_ref[...]
    for g in range(groups):
        acc = jnp.zeros(bk.shape, F32)
        for b in range(NUM_BUCKETS):
            acc = jnp.where(bk == b, tab_ref[b, h * groups + g], acc)
        o_ref[:, 0, :, g * BAND_QB:(g + 1) * BAND_QB] = jnp.where(bk < 0, NEG, acc * LOG2E)


def _band_bias(table, w, dilation, kw, kv_heads):
    groups = N_HEADS // kv_heads
    buckets = _band_buckets(w, dilation, kw)
    return pl.pallas_call(
        functools.partial(_bias_kernel, groups=groups),
        grid=(kv_heads,),
        in_specs=[pl.BlockSpec((3, kw, BAND_QB), lambda h: (0, 0, 0)),
                  pl.BlockSpec(memory_space=pltpu.SMEM)],
        out_specs=pl.BlockSpec((3, 1, kw, groups * BAND_QB), lambda h: (0, h, 0, 0)),
        out_shape=jax.ShapeDtypeStruct((3, kv_heads, kw, groups * BAND_QB), F32),
        compiler_params=_cparams(("parallel",)),
        name="relative_bias_tables",
    )(buckets, table)


def _banded_kernel(q_ref, k_ref, v_ref, tab_ref, *rest, groups, heads, kw, has_sink):
    if has_sink:
        sink_ref, o_ref, lse_ref = rest
    else:
        o_ref, lse_ref = rest
    tq = q_ref.shape[2]
    seq = k_ref.shape[2]
    t0 = pl.program_id(2) * tq
    qb = BAND_QB

    def window(j):
        qoff = pl.multiple_of(j * qb, qb)
        q0 = t0 + qoff
        sel = jnp.where(q0 == 0, 0, jnp.where(q0 == seq - qb, 2, 1))
        ks = pl.multiple_of(jnp.clip(q0 - qb, 0, seq - kw), qb)
        return qoff, sel, ks

    def scores(j, hb):
        qoff, sel, ks = window(j)
        k_t = k_ref[0, hb * HEAD_DIM:(hb + 1) * HEAD_DIM, pl.ds(ks, kw)]
        r0 = hb * groups * HEAD_DIM
        q_cat = jnp.concatenate(
            [q_ref[0, r0 + g * HEAD_DIM:r0 + (g + 1) * HEAD_DIM, pl.ds(qoff, qb)]
             for g in range(groups)], axis=1)
        return _tn_dot(k_t, q_cat) * (HEAD_DIM ** -0.5 * LOG2E) + tab_ref[sel, hb]

    def finish(j, hb, s):
        qoff, _, ks = window(j)
        v_t = v_ref[0, hb * HEAD_DIM:(hb + 1) * HEAD_DIM, pl.ds(ks, kw)]
        m = jnp.max(s, axis=0, keepdims=True)
        if has_sink:
            sink = sink_ref[hb]
            m = jnp.maximum(m, sink)
        p = jnp.exp2(s - m)
        l = jnp.sum(p, axis=0, keepdims=True)
        if has_sink:
            l = l + jnp.exp2(sink - m)
        o = jnp.dot(v_t, p.astype(BF16), preferred_element_type=F32) / l
        return qoff, o.astype(o_ref.dtype), m + jnp.log2(l)

    nsub = tq // qb
    unroll = min(nsub, BAND_UNROLL)

    def sub_blocks(jj, carry):
        units = [(jj * unroll + u, hb) for u in range(unroll) for hb in range(heads)]
        s_vals = [scores(*unit) for unit in units[:BAND_AHEAD]]
        done = []
        for i, unit in enumerate(units):
            if i + BAND_AHEAD < len(units):
                s_vals.append(scores(*units[i + BAND_AHEAD]))
            done.append(finish(*unit, s_vals[i]))
        for (_, hb), (qoff, o, lse) in zip(units, done):
            r0 = hb * groups * HEAD_DIM
            for g in range(groups):
                o_ref[0, r0 + g * HEAD_DIM:r0 + (g + 1) * HEAD_DIM, pl.ds(qoff, qb)] = o[:, g * qb:(g + 1) * qb]
                lse_ref[0, hb, g:g + 1, pl.ds(qoff, qb)] = lse[:, g * qb:(g + 1) * qb]
        return carry

    lax.fori_loop(0, nsub // unroll, sub_blocks, 0)


def _banded_attention(q, k, v, bias, sink, *, kv_heads, heads_per_step, q_row, k_row, v_row, tq):
    b, _, seq = q.shape
    groups = N_HEADS // kv_heads
    hb = heads_per_step
    kw = bias.shape[2]
    qrows, krows = hb * groups * HEAD_DIM, hb * HEAD_DIM
    in_specs = [pl.BlockSpec((1, qrows, tq), lambda i, h, t: (i, q_row // qrows + h, t)),
                pl.BlockSpec((1, krows, seq), lambda i, h, t: (i, k_row // krows + h, 0)),
                pl.BlockSpec((1, krows, seq), lambda i, h, t: (i, v_row // krows + h, 0)),
                pl.BlockSpec((3, hb, kw, groups * BAND_QB), lambda i, h, t: (0, h, 0, 0))]
    args = [q, k, v, bias]
    if sink is not None:
        in_specs.append(pl.BlockSpec((hb, 1, groups * BAND_QB), lambda i, h, t: (h, 0, 0)))
        args.append(sink)
    return pl.pallas_call(
        functools.partial(_banded_kernel, groups=groups, heads=hb, kw=kw, has_sink=sink is not None),
        grid=(b, kv_heads // hb, seq // tq),
        in_specs=in_specs,
        out_specs=[pl.BlockSpec((1, qrows, tq), lambda i, h, t: (i, h, t)),
                   pl.BlockSpec((1, hb, groups, tq), lambda i, h, t: (i, h, 0, t))],
        out_shape=[jax.ShapeDtypeStruct((b, GROUP_WIDTH, seq), BF16),
                   jax.ShapeDtypeStruct((b, kv_heads, groups, seq), F32)],
        compiler_params=_cparams(("parallel", "parallel", "arbitrary")),
        name="banded_attention",
    )(*args)


def _mix_kernel(o1_ref, o2_ref, o3_ref, l1_ref, l2_ref, l3_ref, y_ref):
    for h in range(N_HEADS):
        a1, a2, a3 = l1_ref[0, h], l2_ref[0, h], l3_ref[0, h]
        mx = jnp.maximum(jnp.maximum(a1, a2), a3)
        e1, e2, e3 = jnp.exp2(a1 - mx), jnp.exp2(a2 - mx), jnp.exp2(a3 - mx)
        inv = 1.0 / (e1 + e2 + e3)
        rows = slice(h * HEAD_DIM, (h + 1) * HEAD_DIM)
        y = ((e1 * inv) * o1_ref[0, rows, :].astype(F32) + (e2 * inv) * o2_ref[0, rows, :].astype(F32)
             + (e3 * inv) * o3_ref[0, rows, :].astype(F32))
        y_ref[0, rows, :] = y.astype(BF16)


def _dilated_mix(outs, lses, tm):
    b, _, s = outs[0].shape
    ospec = pl.BlockSpec((1, GROUP_WIDTH, tm), lambda i, t: (i, 0, t))
    lspec = pl.BlockSpec((1, N_HEADS, 1, tm), lambda i, t: (i, 0, 0, t))
    return pl.pallas_call(
        _mix_kernel,
        grid=(b, s // tm),
        in_specs=[ospec] * 3 + [lspec] * 3,
        out_specs=ospec,
        out_shape=jax.ShapeDtypeStruct((b, GROUP_WIDTH, s), BF16),
        compiler_params=_cparams(("parallel", "parallel")),
        name="dilated_mixture",
    )(*outs, *lses)


def _outproj_kernel(ya_ref, yb_ref, yc_ref, yd_ref, z_ref, x_ref, gate_ref, w_ref, g_ref, b_ref, o_ref):
    acc = None
    for i, y_ref in enumerate((ya_ref, yb_ref, yc_ref, yd_ref)):
        rows = slice(i * GROUP_WIDTH, (i + 1) * GROUP_WIDTH)
        z = z_ref[0, rows, :].astype(F32)
        gated = (y_ref[0].astype(F32) * _silu(z)).astype(BF16)
        part = _tn_dot(gated, w_ref[rows, :])
        acc = part if acc is None else acc + part
    r = ALPHA * x_ref[0] + gate_ref[0] * acc
    mu = jnp.mean(r, axis=-1, keepdims=True)
    rc = r - mu
    var = jnp.mean(rc * rc, axis=-1, keepdims=True)
    o_ref[0] = (rc * lax.rsqrt(var + EPS)) * g_ref[...] + b_ref[...]


def _output_projection(ys, proj, x, mod, w_out, ln_g, ln_b, tm):
    b, s, _ = x.shape
    yspec = pl.BlockSpec((1, GROUP_WIDTH, tm), lambda i, t: (i, 0, t))
    return pl.pallas_call(
        _outproj_kernel,
        grid=(b, s // tm),
        in_specs=[yspec] * 4 + [
            pl.BlockSpec((1, D_MIX, tm), lambda i, t: (i, Z_ROW // D_MIX, t)),
            pl.BlockSpec((1, tm, D_MODEL), lambda i, t: (i, t, 0)),
            pl.BlockSpec((1, 1, D_MODEL), lambda i, t: (i, 0, 2)),
            pl.BlockSpec((D_MIX, D_MODEL), lambda i, t: (0, 0)),
            pl.BlockSpec((1, D_MODEL), lambda i, t: (0, 0)),
            pl.BlockSpec((1, D_MODEL), lambda i, t: (0, 0))],
        out_specs=pl.BlockSpec((1, tm, D_MODEL), lambda i, t: (i, t, 0)),
        out_shape=jax.ShapeDtypeStruct((b, s, D_MODEL), F32),
        compiler_params=_cparams(("parallel", "parallel")),
        name="output_projection",
    )(*ys, proj, x, mod, w_out, ln_g, ln_b)


def _rope_tables(seq):
    pos = jnp.arange(seq)
    inv = ROPE_THETA ** (-jnp.arange(0, A_ROPE, 2, dtype=F32) / A_ROPE)

    def cs(p):
        ang = p.astype(F32)[:, None] * inv[None, :]
        return jnp.cos(ang).T, jnp.sin(ang).T

    c, sn = cs(pos)
    ones, zeros = jnp.ones((A_NOPE, seq), F32), jnp.zeros((A_NOPE, seq), F32)
    pad = jnp.zeros((A_QK_PAD - A_NOPE - A_ROPE, seq), F32)
    a_c = jnp.concatenate([ones, c, c, pad], axis=0)
    a_s = jnp.concatenate([zeros, -sn, sn, pad], axis=0)
    cr, sr = cs(pos // GRID_W)
    cc, sc = cs(pos % GRID_W)
    d_c = jnp.concatenate([cr, cr, cc, cc], axis=0)
    d_s = jnp.concatenate([-sr, sr, -sc, sc], axis=0)
    return a_c, a_s, d_c, d_s


def _layer_weights(l, w_in, a_q_norm, a_q_up, a_kv_norm, a_kv_up, b_sink, d_q_norm, d_k_norm, w_out,
                   ln_g, ln_b):
    wa, wb, wc, wd, wz = jnp.split(w_in[l], np.cumsum([A_COLS, B_COLS, C_COLS, D_COLS]).tolist(), axis=1)
    wa = jnp.pad(wa, ((0, 0), (0, A_ROWS - A_COLS)))
    w_t = jnp.concatenate([wz, wa, wb, wd, wc], axis=1).T.astype(BF16)
    dqk = A_NOPE + A_ROPE
    wq = jnp.pad(a_q_up[l].reshape(A_Q_LORA, N_HEADS, dqk), ((0, 0), (0, 0), (0, A_QK_PAD - dqk)))
    wq_t = wq.reshape(A_Q_LORA, N_HEADS * A_QK_PAD).T.astype(BF16)
    kvu = a_kv_up[l].reshape(A_KV_LORA, N_HEADS, A_NOPE + HEAD_DIM)
    wk_top = jnp.pad(kvu[:, :, :A_NOPE], ((0, 0), (0, 0), (0, A_QK_PAD - A_NOPE)))
    sel = jnp.zeros((A_ROPE, N_HEADS, A_QK_PAD), F32)
    sel = sel.at[:, :, A_NOPE:A_NOPE + A_ROPE].set(jnp.eye(A_ROPE, dtype=F32)[:, None, :])
    wk_t = jnp.concatenate([wk_top, sel], axis=0).reshape(A_KV_LORA + A_ROPE, -1).T.astype(BF16)
    wv_t = kvu[:, :, A_NOPE:].reshape(A_KV_LORA, GROUP_WIDTH).T.astype(BF16)
    d_gain = jnp.concatenate([jnp.broadcast_to(d_q_norm[l], (N_HEADS, HEAD_DIM)),
                              jnp.broadcast_to(d_k_norm[l], (D_KV_HEADS, HEAD_DIM))])[:, :, None]
    groups = N_HEADS // B_KV_HEADS
    sink = jnp.repeat(b_sink[l].reshape(B_KV_HEADS, 1, groups) * LOG2E, BAND_QB, axis=2)
    return dict(w_t=w_t, gq=a_q_norm[l][:, None], gkv=a_kv_norm[l][:, None], wq_t=wq_t, wk_t=wk_t,
                wv_t=wv_t, d_gain=d_gain, sink=sink, w_out=w_out[l].astype(BF16),
                ln_g=ln_g[l][None, :], ln_b=ln_b[l][None, :])


def _to_strided(t, r):
    b, f, s = t.shape
    return t.reshape(b, f, s // r, r).transpose(0, 3, 1, 2).reshape(b * r, f, s // r)


def _from_strided(t, b, r):
    rest = t.shape[1:-1]
    sr = t.shape[-1]
    t = jnp.moveaxis(t.reshape((b, r) + rest + (sr,)), 1, -1)
    return t.reshape((b,) + rest + (sr * r,))


def _pick(n, pref):
    return min(n, pref)


def _layer(x, mod, wts, bias_b, bias_c, ropes):
    b, s, _ = x.shape
    a_c, a_s, d_c, d_s = ropes
    proj = _input_projection(x, mod, wts["w_t"], tm=_pick(s, 1024), tn=1088)

    tp = _pick(s, 512)
    q_a, k_a, v_a = _mla_prep(proj, wts["gq"], wts["gkv"], wts["wq_t"], wts["wk_t"], wts["wv_t"],
                              a_c, a_s, tp)
    y_a = _dense_attention(q_a, k_a, v_a, kv_heads=N_HEADS, dq=A_QK_PAD, k_blk0=0, v_blk0=0,
                           tq=512, nsub=2, tk=512)

    y_b, _ = _banded_attention(proj, proj, proj, bias_b, wts["sink"], kv_heads=B_KV_HEADS,
                               heads_per_step=1, q_row=B_ROW, k_row=B_ROW + GROUP_WIDTH,
                               v_row=B_ROW + GROUP_WIDTH + B_KV_HEADS * HEAD_DIM, tq=_pick(s, 1024))

    outs, lses = [], []
    for (window, r), bias in zip(DILATED_PAIRS, bias_c):
        if r == 1:
            o, lse = _banded_attention(proj, proj, proj, bias, None, kv_heads=N_HEADS, heads_per_step=2,
                                       q_row=C_ROW, k_row=C_ROW + GROUP_WIDTH,
                                       v_row=C_ROW + 2 * GROUP_WIDTH, tq=_pick(s, 1024))
        else:
            qkv = _to_strided(proj[:, C_ROW:, :], r)
            o, lse = _banded_attention(qkv, qkv, qkv, bias, None, kv_heads=N_HEADS, heads_per_step=4,
                                       q_row=0, k_row=GROUP_WIDTH, v_row=2 * GROUP_WIDTH,
                                       tq=_pick(s // r, 1024))
            o, lse = _from_strided(o, b, r), _from_strided(lse, b, r)
        outs.append(o)
        lses.append(lse)
    y_c = _dilated_mix(outs, lses, _pick(s, 1024))

    q_d, k_d = _axial_prep(proj, wts["d_gain"], d_c, d_s, tp)
    y_d = _dense_attention(q_d, k_d, proj, kv_heads=D_KV_HEADS, dq=HEAD_DIM, k_blk0=0,
                           v_blk0=(D_ROW + GROUP_WIDTH + D_KV_HEADS * HEAD_DIM) // HEAD_DIM,
                           tq=512, nsub=1, tk=512)

    return _output_projection((y_a, y_b, y_c, y_d), proj, x, mod, wts["w_out"], wts["ln_g"], wts["ln_b"],
                              tm=_pick(s, 256))


def _band_kw(seq):
    return min(seq, BAND_QB + 2 * LANE)


def _trunk(x, mods, layer_wts, rel_bias):
    b, s, _ = x.shape
    ropes = _rope_tables(s)
    bias_b = _band_bias(rel_bias[:, :N_HEADS], B_WINDOW, 1, _band_kw(s), B_KV_HEADS)
    bias_c = [_band_bias(rel_bias[:, N_HEADS:], window // (2 * r), r, _band_kw(s // r), N_HEADS)
              for window, r in DILATED_PAIRS]
    for l in range(DEPTH):
        x = _layer(x, mods[l], layer_wts[l], bias_b, bias_c, ropes)
    return x


def kernel(x_prompt, x_sample, c_prompt, c_sample, w_ada, b_ada, w_in, a_q_norm, a_q_up, a_kv_norm, a_kv_up, b_sink, d_q_norm, d_k_norm, w_out, ln_g, ln_b, rel_bias):
    bp, bs = c_prompt.shape[0], c_sample.shape[0]
    rows = -(-(bp + bs) // 16) * 16
    c_all = jnp.pad(jnp.concatenate([c_prompt, c_sample], axis=0), ((0, rows - bp - bs), (0, 0)))
    mod = _modulation(c_all, w_ada, b_ada)
    layer_wts = [_layer_weights(l, w_in, a_q_norm, a_q_up, a_kv_norm, a_kv_up, b_sink, d_q_norm,
                                d_k_norm, w_out, ln_g, ln_b) for l in range(DEPTH)]
    mods_p = [mod[l, :bp].reshape(bp, 1, -1) for l in range(DEPTH)]
    mods_s = [mod[l, bp:bp + bs].reshape(bs, 1, -1) for l in range(DEPTH)]
    y_prompt = _trunk(x_prompt, mods_p, layer_wts, rel_bias)
    y_sample = _trunk(x_sample, mods_s, layer_wts, rel_bias)
    return (y_prompt, y_sample)
```

```python
import functools
import math

import numpy as np
import jax
import jax.numpy as jnp
from jax import lax
from jax.experimental import pallas as pl
from jax.experimental.pallas import tpu as pltpu

F32 = jnp.float32
BF16 = jnp.bfloat16

D_MODEL = 2048
DEPTH = 2
HEAD_DIM = 64
N_HEADS = 12
A_Q_LORA = 384
A_KV_LORA = 256
A_NOPE = 64
A_ROPE = 32
A_QK_PAD = 128
B_KV_HEADS = 4
B_WINDOW = 128
DILATED_PAIRS = ((128, 1), (512, 4), (2048, 16))
D_KV_HEADS = 4
GROUP_WIDTH = N_HEADS * HEAD_DIM
D_MIX = 4 * GROUP_WIDTH
A_COLS = A_Q_LORA + A_KV_LORA + A_ROPE
B_COLS = (N_HEADS + 2 * B_KV_HEADS) * HEAD_DIM
C_COLS = 3 * N_HEADS * HEAD_DIM
D_COLS = (N_HEADS + 2 * D_KV_HEADS) * HEAD_DIM
GRID_W = 64
NUM_BUCKETS = 32
T5_MAX_DISTANCE = 1024
ROPE_THETA = 10000.0
EPS = 1e-6
NEG = -1e30
ALPHA = (2 * DEPTH) ** 0.25
LOG2E = 1.4426950408889634

Z_ROW = 0
A_ROW = D_MIX
A_ROWS = 768
B_ROW = A_ROW + A_ROWS
D_ROW = B_ROW + B_COLS
C_ROW = D_ROW + D_COLS
N_PROJ = C_ROW + C_COLS

LANE = 128
BAND_QB = 128
BAND_QB_WIDE = 256
BAND_HALO = 128
BAND_UNROLL = 4
BAND_AHEAD = 3
VMEM_LIMIT = 52 * 2 ** 20


def _cparams(sem):
    return pltpu.CompilerParams(dimension_semantics=sem, vmem_limit_bytes=VMEM_LIMIT)


def _silu(x):
    return x / (1.0 + jnp.exp(-x))


def _tn_dot(a, b):
    return lax.dot_general(a, b, (((0,), (0,)), ((), ())), preferred_element_type=F32)


def _nt_dot(a, b):
    return lax.dot_general(a, b, (((1,), (1,)), ((), ())), preferred_element_type=F32)


def _mod_kernel(c_ref, w_ref, b_ref, o_ref):
    sc = _silu(c_ref[...]).astype(BF16)
    o_ref[0] = jnp.dot(sc, w_ref[0].astype(BF16), preferred_element_type=F32) + b_ref[0]


def _modulation(c_all, w_ada, b_ada):
    depth, _, n = w_ada.shape
    rows = c_all.shape[0]
    tn = 768
    return pl.pallas_call(
        _mod_kernel,
        grid=(depth, n // tn),
        in_specs=[pl.BlockSpec((rows, D_MODEL), lambda l, j: (0, 0)),
                  pl.BlockSpec((1, D_MODEL, tn), lambda l, j: (l, 0, j)),
                  pl.BlockSpec((1, 1, tn), lambda l, j: (l, 0, j))],
        out_specs=pl.BlockSpec((1, rows, tn), lambda l, j: (l, 0, j)),
        out_shape=jax.ShapeDtypeStruct((depth, rows, n), F32),
        compiler_params=_cparams(("parallel", "parallel")),
        name="adaln_modulation",
    )(c_all, w_ada, b_ada.reshape(depth, 1, n))


def _inproj_kernel(x_ref, sh_ref, sc_ref, w_ref, o_ref, h_ref):
    @pl.when(pl.program_id(2) == 0)
    def _():
        x = x_ref[0]
        mu = jnp.mean(x, axis=-1, keepdims=True)
        xc = x - mu
        var = jnp.mean(xc * xc, axis=-1, keepdims=True)
        h = (xc * lax.rsqrt(var + EPS)) * (1.0 + sc_ref[0]) + sh_ref[0]
        h_ref[...] = h.astype(BF16)

    o_ref[0] = _nt_dot(w_ref[...], h_ref[...]).astype(BF16)


def _input_projection(x, mod, w_t, tm, tn):
    b, s, _ = x.shape
    n = w_t.shape[0]
    return pl.pallas_call(
        _inproj_kernel,
        grid=(b, s // tm, n // tn),
        in_specs=[pl.BlockSpec((1, tm, D_MODEL), lambda i, t, j: (i, t, 0)),
                  pl.BlockSpec((1, 1, D_MODEL), lambda i, t, j: (i, 0, 0)),
                  pl.BlockSpec((1, 1, D_MODEL), lambda i, t, j: (i, 0, 1)),
                  pl.BlockSpec((tn, D_MODEL), lambda i, t, j: (j, 0))],
        out_specs=pl.BlockSpec((1, tn, tm), lambda i, t, j: (i, j, t)),
        out_shape=jax.ShapeDtypeStruct((b, n, s), BF16),
        scratch_shapes=[pltpu.VMEM((tm, D_MODEL), BF16)],
        compiler_params=_cparams(("parallel", "parallel", "arbitrary")),
        name="input_projection",
    )(x, mod, mod, w_t)


def _rms_rows(x, g):
    return x * lax.rsqrt(jnp.mean(x * x, axis=0, keepdims=True) + EPS) * g


def _mla_prep_kernel(a_ref, gq_ref, gkv_ref, wq_ref, wk_ref, wv_ref, rc_ref, rs_ref,
                     q_ref, k_ref, v_ref, *, q_scale):
    a = a_ref[0].astype(F32)
    tm = a.shape[1]
    qn = _rms_rows(a[0:A_Q_LORA], gq_ref[...]).astype(BF16)
    kvn = _rms_rows(a[A_Q_LORA:A_Q_LORA + A_KV_LORA], gkv_ref[...])
    kpe = a[A_Q_LORA + A_KV_LORA:A_COLS]
    rc = rc_ref[...]
    rs = rs_ref[...]
    half = A_ROPE // 2
    lo, mid, hi = A_NOPE, A_NOPE + half, A_NOPE + A_ROPE

    q = jnp.dot(wq_ref[...], qn, preferred_element_type=F32)
    q = q.reshape(N_HEADS, A_QK_PAD, tm)
    q_sw = jnp.concatenate([q[:, :lo], q[:, mid:hi], q[:, lo:mid], q[:, hi:]], axis=1)
    q = (q * rc[None] + q_sw * rs[None]) * q_scale
    q_ref[0] = q.reshape(N_HEADS * A_QK_PAD, tm).astype(BF16)

    kpe_sw = jnp.concatenate([kpe[half:], kpe[:half]], axis=0)
    kpe = kpe * rc[lo:hi] + kpe_sw * rs[lo:hi]
    xk = jnp.concatenate([kvn, kpe], axis=0).astype(BF16)
    k_ref[0] = jnp.dot(wk_ref[...], xk, preferred_element_type=F32).astype(BF16)
    v_ref[0] = jnp.dot(wv_ref[...], xk[:A_KV_LORA], preferred_element_type=F32).astype(BF16)


def _mla_prep(proj, gq, gkv, wq_t, wk_t, wv_t, rope_c, rope_s, tm):
    b, _, s = proj.shape
    hq = N_HEADS * A_QK_PAD
    full = lambda shape: pl.BlockSpec(shape, lambda i, t: (0,) * len(shape))
    return pl.pallas_call(
        functools.partial(_mla_prep_kernel, q_scale=(A_NOPE + A_ROPE) ** -0.5 * LOG2E),
        grid=(b, s // tm),
        in_specs=[pl.BlockSpec((1, A_ROWS, tm), lambda i, t: (i, A_ROW // A_ROWS, t)),
                  full((A_Q_LORA, 1)), full((A_KV_LORA, 1)),
                  full(wq_t.shape), full(wk_t.shape), full(wv_t.shape),
                  pl.BlockSpec((A_QK_PAD, tm), lambda i, t: (0, t)),
                  pl.BlockSpec((A_QK_PAD, tm), lambda i, t: (0, t))],
        out_specs=[pl.BlockSpec((1, hq, tm), lambda i, t: (i, 0, t)),
                   pl.BlockSpec((1, hq, tm), lambda i, t: (i, 0, t)),
                   pl.BlockSpec((1, GROUP_WIDTH, tm), lambda i, t: (i, 0, t))],
        out_shape=[jax.ShapeDtypeStruct((b, hq, s), BF16),
                   jax.ShapeDtypeStruct((b, hq, s), BF16),
                   jax.ShapeDtypeStruct((b, GROUP_WIDTH, s), BF16)],
        compiler_params=_cparams(("parallel", "parallel")),
        name="latent_attention_prep",
    )(proj, gq, gkv, wq_t, wk_t, wv_t, rope_c, rope_s)


def _axial_prep_kernel(x_ref, g_ref, rc_ref, rs_ref, q_ref, k_ref, *, q_scale):
    x = x_ref[0].astype(F32)
    tm = x.shape[1]
    nh = N_HEADS + D_KV_HEADS
    x = x.reshape(nh, HEAD_DIM, tm)
    x = x * lax.rsqrt(jnp.mean(x * x, axis=1, keepdims=True) + EPS) * g_ref[...]
    q4 = HEAD_DIM // 4
    x_sw = jnp.concatenate([x[:, q4:2 * q4], x[:, :q4], x[:, 3 * q4:], x[:, 2 * q4:3 * q4]], axis=1)
    x = x * rc_ref[...][None] + x_sw * rs_ref[...][None]
    q_ref[0] = (x[:N_HEADS] * q_scale).reshape(GROUP_WIDTH, tm).astype(BF16)
    k_ref[0] = x[N_HEADS:].reshape(D_KV_HEADS * HEAD_DIM, tm).astype(BF16)


def _axial_prep(proj, gains, rope_c, rope_s, tm):
    b, _, s = proj.shape
    rows = GROUP_WIDTH + D_KV_HEADS * HEAD_DIM
    return pl.pallas_call(
        functools.partial(_axial_prep_kernel, q_scale=HEAD_DIM ** -0.5 * LOG2E),
        grid=(b, s // tm),
        in_specs=[pl.BlockSpec((1, rows, tm), lambda i, t: (i, D_ROW // rows, t)),
                  pl.BlockSpec((N_HEADS + D_KV_HEADS, HEAD_DIM, 1), lambda i, t: (0, 0, 0)),
                  pl.BlockSpec((HEAD_DIM, tm), lambda i, t: (0, t)),
                  pl.BlockSpec((HEAD_DIM, tm), lambda i, t: (0, t))],
        out_specs=[pl.BlockSpec((1, GROUP_WIDTH, tm), lambda i, t: (i, 0, t)),
                   pl.BlockSpec((1, D_KV_HEADS * HEAD_DIM, tm), lambda i, t: (i, 0, t))],
        out_shape=[jax.ShapeDtypeStruct((b, GROUP_WIDTH, s), BF16),
                   jax.ShapeDtypeStruct((b, D_KV_HEADS * HEAD_DIM, s), BF16)],
        compiler_params=_cparams(("parallel", "parallel")),
        name="axial_attention_prep",
    )(proj, gains, rope_c, rope_s)


V_EXT = HEAD_DIM + 16
FLASH_CHUNKS = 4


def _flash_kernel(q_ref, k_ref, v_ref, o_ref, s_ref, *, groups, nsub, dq, tk):
    tq = q_ref.shape[2] // nsub
    nk = k_ref.shape[2] // tk
    chains = [(g, u) for g in range(groups) for u in range(nsub)]
    ones = jnp.ones((V_EXT - HEAD_DIM, tk), BF16)

    def scores(c, ci, slot):
        g, u = chains[ci]
        off = pl.multiple_of(c * tk, tk)
        s = _tn_dot(k_ref[0, :, pl.ds(off, tk)], q_ref[0, g * dq:(g + 1) * dq, u * tq:(u + 1) * tq])
        s_ref[ci, slot] = s
        return jnp.max(s, axis=0, keepdims=True)

    def consume(c, ci, slot, cmax, m, acc):
        off = pl.multiple_of(c * tk, tk)
        v_ext = jnp.concatenate([v_ref[0, :, pl.ds(off, tk)], ones], axis=0)
        m_new = jnp.maximum(m, cmax)
        alpha = jnp.exp2(m - m_new)
        p = jnp.exp2(s_ref[ci, slot] - m_new).astype(BF16)
        acc = alpha * acc + jnp.dot(v_ext, p, preferred_element_type=F32)
        return m_new, acc

    def chunk_group(c0, carry, last):
        state = list(carry)
        for t in range(FLASH_CHUNKS):
            for ci, (cmax, m, acc) in enumerate(state):
                cmax_next = cmax
                if not (last and t == FLASH_CHUNKS - 1):
                    cmax_next = scores(c0 + t + 1, ci, (t + 1) % 2)
                m, acc = consume(c0 + t, ci, t % 2, cmax, m, acc)
                state[ci] = (cmax_next, m, acc)
        return tuple(state)

    init = tuple((scores(0, ci, 0), jnp.full((1, tq), NEG, F32), jnp.zeros((V_EXT, tq), F32))
                 for ci in range(len(chains)))
    carry = lax.fori_loop(0, nk // FLASH_CHUNKS - 1,
                          lambda i, carry: chunk_group(i * FLASH_CHUNKS, carry, False), init)
    carry = chunk_group(nk - FLASH_CHUNKS, carry, True)
    for (g, u), (_, _, acc) in zip(chains, carry):
        o_ref[0, g * HEAD_DIM:(g + 1) * HEAD_DIM, u * tq:(u + 1) * tq] = (
            acc[:HEAD_DIM] / acc[HEAD_DIM:HEAD_DIM + 1]).astype(BF16)


def _dense_attention(q, k, v, *, kv_heads, dq, k_blk0, v_blk0, tq, nsub, tk):
    b, _, s = q.shape
    groups = N_HEADS // kv_heads
    assert FLASH_CHUNKS % 2 == 0 and (s // tk) % FLASH_CHUNKS == 0 and s % (tq * nsub) == 0
    return pl.pallas_call(
        functools.partial(_flash_kernel, groups=groups, nsub=nsub, dq=dq, tk=tk),
        grid=(b, kv_heads, s // (tq * nsub)),
        in_specs=[pl.BlockSpec((1, groups * dq, tq * nsub), lambda i, h, t: (i, h, t)),
                  pl.BlockSpec((1, dq, s), lambda i, h, t: (i, k_blk0 + h, 0)),
                  pl.BlockSpec((1, HEAD_DIM, s), lambda i, h, t: (i, v_blk0 + h, 0))],
        out_specs=pl.BlockSpec((1, groups * HEAD_DIM, tq * nsub), lambda i, h, t: (i, h, t)),
        out_shape=jax.ShapeDtypeStruct((b, GROUP_WIDTH, s), BF16),
        scratch_shapes=[pltpu.VMEM((groups * nsub, 2, tk, tq), F32)],
        compiler_params=_cparams(("parallel", "parallel", "arbitrary")),
        name="dense_attention",
    )(q, k, v)


def _t5_bucket(rel):
    half = NUM_BUCKETS // 2
    exact = half // 2
    n = jnp.abs(rel)
    large = exact + (jnp.log(jnp.maximum(n, 1).astype(F32) / exact)
                     / math.log(T5_MAX_DISTANCE / exact) * (half - exact)).astype(jnp.int32)
    large = jnp.minimum(large, half - 1)
    return jnp.where(rel > 0, half, 0) + jnp.where(n < exact, n, large)


def _band_buckets(w, dilation, kw, qb):
    kk = jnp.arange(kw)[:, None]
    qq = jnp.arange(qb)[None, :]
    out = []
    for shift in (0, BAND_HALO, kw - qb):
        rel = kk - shift - qq
        out.append(jnp.where(jnp.abs(rel) <= w, _t5_bucket(dilation * rel), -1))
    return jnp.stack(out).astype(jnp.int32)


def _bias_kernel(bucket_ref, tab_ref, o_ref, *, groups):
    h = pl.program_id(0)
    bk = bucket_ref[...]
    qb = bk.shape[2]
    for g in range(groups):
        acc = jnp.zeros(bk.shape, F32)
        for b in range(NUM_BUCKETS):
            acc = jnp.where(bk == b, tab_ref[b, h * groups + g], acc)
        o_ref[:, 0, :, g * qb:(g + 1) * qb] = jnp.where(bk < 0, NEG, acc * LOG2E)


def _band_bias(table, w, dilation, seq, qb, kv_heads):
    groups = N_HEADS // kv_heads
    kw = min(seq, qb + 2 * BAND_HALO)
    buckets = _band_buckets(w, dilation, kw, qb)
    return pl.pallas_call(
        functools.partial(_bias_kernel, groups=groups),
        grid=(kv_heads,),
        in_specs=[pl.BlockSpec((3, kw, qb), lambda h: (0, 0, 0)),
                  pl.BlockSpec(memory_space=pltpu.SMEM)],
        out_specs=pl.BlockSpec((3, 1, kw, groups * qb), lambda h: (0, h, 0, 0)),
        out_shape=jax.ShapeDtypeStruct((3, kv_heads, kw, groups * qb), F32),
        compiler_params=_cparams(("parallel",)),
        name="relative_bias_tables",
    )(buckets, table)


def _banded_kernel(q_ref, k_ref, v_ref, tab_ref, *rest, groups, heads, kw, has_sink):
    if has_sink:
        sink_ref, o_ref, lse_ref = rest
    else:
        o_ref, lse_ref = rest
    tq = q_ref.shape[2]
    seq = k_ref.shape[2]
    t0 = pl.program_id(2) * tq
    qb = tab_ref.shape[3] // groups

    def window(j):
        qoff = pl.multiple_of(j * qb, qb)
        q0 = t0 + qoff
        sel = jnp.where(q0 == 0, 0, jnp.where(q0 == seq - qb, 2, 1))
        ks = pl.multiple_of(jnp.clip(q0 - BAND_HALO, 0, seq - kw), BAND_HALO)
        return qoff, sel, ks

    def scores(j, hb):
        qoff, sel, ks = window(j)
        k_t = k_ref[0, hb * HEAD_DIM:(hb + 1) * HEAD_DIM, pl.ds(ks, kw)]
        r0 = hb * groups * HEAD_DIM
        q_cat = jnp.concatenate(
            [q_ref[0, r0 + g * HEAD_DIM:r0 + (g + 1) * HEAD_DIM, pl.ds(qoff, qb)]
             for g in range(groups)], axis=1)
        return _tn_dot(k_t, q_cat) + tab_ref[sel, hb]

    ones = jnp.ones((V_EXT - HEAD_DIM, kw), BF16)

    def finish(j, hb, s):
        qoff, _, ks = window(j)
        v_ext = jnp.concatenate([v_ref[0, hb * HEAD_DIM:(hb + 1) * HEAD_DIM, pl.ds(ks, kw)], ones], axis=0)
        m = jnp.max(s, axis=0, keepdims=True)
        if has_sink:
            sink = sink_ref[hb]
            m = jnp.maximum(m, sink)
        p = jnp.exp2(s - m).astype(BF16)
        ol = jnp.dot(v_ext, p, preferred_element_type=F32)
        l = ol[HEAD_DIM:HEAD_DIM + 1]
        if has_sink:
            l = l + jnp.exp2(sink - m)
        return qoff, (ol[:HEAD_DIM] / l).astype(o_ref.dtype), m + jnp.log2(l)

    nsub = tq // qb
    unroll = min(nsub, BAND_UNROLL)

    def sub_blocks(jj, carry):
        units = [(jj * unroll + u, hb) for u in range(unroll) for hb in range(heads)]
        s_vals = [scores(*unit) for unit in units[:BAND_AHEAD]]
        done = []
        for i, unit in enumerate(units):
            if i + BAND_AHEAD < len(units):
                s_vals.append(scores(*units[i + BAND_AHEAD]))
            done.append(finish(*unit, s_vals[i]))
        for (_, hb), (qoff, o, lse) in zip(units, done):
            r0 = hb * groups * HEAD_DIM
            for g in range(groups):
                o_ref[0, r0 + g * HEAD_DIM:r0 + (g + 1) * HEAD_DIM, pl.ds(qoff, qb)] = o[:, g * qb:(g + 1) * qb]
                lse_ref[0, hb, g:g + 1, pl.ds(qoff, qb)] = lse[:, g * qb:(g + 1) * qb]
        return carry

    lax.fori_loop(0, nsub // unroll, sub_blocks, 0)


def _banded_attention(q, k, v, bias, sink, *, kv_heads, heads_per_step, q_row, k_row, v_row, tq):
    b, _, seq = q.shape
    groups = N_HEADS // kv_heads
    hb = heads_per_step
    kw = bias.shape[2]
    qrows, krows = hb * groups * HEAD_DIM, hb * HEAD_DIM
    in_specs = [pl.BlockSpec((1, qrows, tq), lambda i, h, t: (i, q_row // qrows + h, t)),
                pl.BlockSpec((1, krows, seq), lambda i, h, t: (i, k_row // krows + h, 0)),
                pl.BlockSpec((1, krows, seq), lambda i, h, t: (i, v_row // krows + h, 0)),
                pl.BlockSpec((3, hb, kw, bias.shape[3]), lambda i, h, t: (0, h, 0, 0))]
    args = [q, k, v, bias]
    if sink is not None:
        in_specs.append(pl.BlockSpec((hb, 1, bias.shape[3]), lambda i, h, t: (h, 0, 0)))
        args.append(sink)
    return pl.pallas_call(
        functools.partial(_banded_kernel, groups=groups, heads=hb, kw=kw, has_sink=sink is not None),
        grid=(b, kv_heads // hb, seq // tq),
        in_specs=in_specs,
        out_specs=[pl.BlockSpec((1, qrows, tq), lambda i, h, t: (i, h, t)),
                   pl.BlockSpec((1, hb, groups, tq), lambda i, h, t: (i, h, 0, t))],
        out_shape=[jax.ShapeDtypeStruct((b, GROUP_WIDTH, seq), BF16),
                   jax.ShapeDtypeStruct((b, kv_heads, groups, seq), F32)],
        compiler_params=_cparams(("parallel", "parallel", "arbitrary")),
        name="banded_attention",
    )(*args)


def _mix_kernel(o1_ref, o2_ref, o3_ref, l1_ref, l2_ref, l3_ref, y_ref):
    for h in range(N_HEADS):
        a1, a2, a3 = l1_ref[0, h], l2_ref[0, h], l3_ref[0, h]
        mx = jnp.maximum(jnp.maximum(a1, a2), a3)
        e1, e2, e3 = jnp.exp2(a1 - mx), jnp.exp2(a2 - mx), jnp.exp2(a3 - mx)
        inv = 1.0 / (e1 + e2 + e3)
        rows = slice(h * HEAD_DIM, (h + 1) * HEAD_DIM)
        y = ((e1 * inv) * o1_ref[0, rows, :].astype(F32) + (e2 * inv) * o2_ref[0, rows, :].astype(F32)
             + (e3 * inv) * o3_ref[0, rows, :].astype(F32))
        y_ref[0, rows, :] = y.astype(BF16)


def _dilated_mix(outs, lses, tm):
    b, _, s = outs[0].shape
    ospec = pl.BlockSpec((1, GROUP_WIDTH, tm), lambda i, t: (i, 0, t))
    lspec = pl.BlockSpec((1, N_HEADS, 1, tm), lambda i, t: (i, 0, 0, t))
    return pl.pallas_call(
        _mix_kernel,
        grid=(b, s // tm),
        in_specs=[ospec] * 3 + [lspec] * 3,
        out_specs=ospec,
        out_shape=jax.ShapeDtypeStruct((b, GROUP_WIDTH, s), BF16),
        compiler_params=_cparams(("parallel", "parallel")),
        name="dilated_mixture",
    )(*outs, *lses)


def _outproj_kernel(ya_ref, yb_ref, yc_ref, yd_ref, z_ref, x_ref, gate_ref, w_ref, g_ref, b_ref, o_ref):
    acc = None
    for i, y_ref in enumerate((ya_ref, yb_ref, yc_ref, yd_ref)):
        rows = slice(i * GROUP_WIDTH, (i + 1) * GROUP_WIDTH)
        z = z_ref[0, rows, :].astype(F32)
        gated = (y_ref[0].astype(F32) * _silu(z)).astype(BF16)
        part = _tn_dot(gated, w_ref[rows, :])
        acc = part if acc is None else acc + part
    r = ALPHA * x_ref[0] + gate_ref[0] * acc
    mu = jnp.mean(r, axis=-1, keepdims=True)
    rc = r - mu
    var = jnp.mean(rc * rc, axis=-1, keepdims=True)
    o_ref[0] = (rc * lax.rsqrt(var + EPS)) * g_ref[...] + b_ref[...]


def _output_projection(ys, proj, x, mod, w_out, ln_g, ln_b, tm):
    b, s, _ = x.shape
    yspec = pl.BlockSpec((1, GROUP_WIDTH, tm), lambda i, t: (i, 0, t))
    return pl.pallas_call(
        _outproj_kernel,
        grid=(b, s // tm),
        in_specs=[yspec] * 4 + [
            pl.BlockSpec((1, D_MIX, tm), lambda i, t: (i, Z_ROW // D_MIX, t)),
            pl.BlockSpec((1, tm, D_MODEL), lambda i, t: (i, t, 0)),
            pl.BlockSpec((1, 1, D_MODEL), lambda i, t: (i, 0, 2)),
            pl.BlockSpec((D_MIX, D_MODEL), lambda i, t: (0, 0)),
            pl.BlockSpec((1, D_MODEL), lambda i, t: (0, 0)),
            pl.BlockSpec((1, D_MODEL), lambda i, t: (0, 0))],
        out_specs=pl.BlockSpec((1, tm, D_MODEL), lambda i, t: (i, t, 0)),
        out_shape=jax.ShapeDtypeStruct((b, s, D_MODEL), F32),
        compiler_params=_cparams(("parallel", "parallel")),
        name="output_projection",
    )(*ys, proj, x, mod, w_out, ln_g, ln_b)


def _rope_tables(seq):
    pos = jnp.arange(seq)
    inv = ROPE_THETA ** (-jnp.arange(0, A_ROPE, 2, dtype=F32) / A_ROPE)

    def cs(p):
        ang = p.astype(F32)[:, None] * inv[None, :]
        return jnp.cos(ang).T, jnp.sin(ang).T

    c, sn = cs(pos)
    ones, zeros = jnp.ones((A_NOPE, seq), F32), jnp.zeros((A_NOPE, seq), F32)
    pad = jnp.zeros((A_QK_PAD - A_NOPE - A_ROPE, seq), F32)
    a_c = jnp.concatenate([ones, c, c, pad], axis=0)
    a_s = jnp.concatenate([zeros, -sn, sn, pad], axis=0)
    cr, sr = cs(pos // GRID_W)
    cc, sc = cs(pos % GRID_W)
    d_c = jnp.concatenate([cr, cr, cc, cc], axis=0)
    d_s = jnp.concatenate([-sr, sr, -sc, sc], axis=0)
    return a_c, a_s, d_c, d_s


def _layer_weights(l, w_in, a_q_norm, a_q_up, a_kv_norm, a_kv_up, b_sink, d_q_norm, d_k_norm, w_out,
                   ln_g, ln_b):
    wa, wb, wc, wd, wz = jnp.split(w_in[l], np.cumsum([A_COLS, B_COLS, C_COLS, D_COLS]).tolist(), axis=1)
    wa = jnp.pad(wa, ((0, 0), (0, A_ROWS - A_COLS)))
    band_scale = jnp.concatenate([jnp.full((GROUP_WIDTH,), HEAD_DIM ** -0.5 * LOG2E, F32),
                                  jnp.ones((B_COLS - GROUP_WIDTH,), F32)])
    wb = wb * band_scale[None, :]
    wc = wc * jnp.pad(band_scale[:GROUP_WIDTH], (0, C_COLS - GROUP_WIDTH), constant_values=1.0)[None, :]
    w_t = jnp.concatenate([wz, wa, wb, wd, wc], axis=1).T.astype(BF16)
    dqk = A_NOPE + A_ROPE
    wq = jnp.pad(a_q_up[l].reshape(A_Q_LORA, N_HEADS, dqk), ((0, 0), (0, 0), (0, A_QK_PAD - dqk)))
    wq_t = wq.reshape(A_Q_LORA, N_HEADS * A_QK_PAD).T.astype(BF16)
    kvu = a_kv_up[l].reshape(A_KV_LORA, N_HEADS, A_NOPE + HEAD_DIM)
    wk_top = jnp.pad(kvu[:, :, :A_NOPE], ((0, 0), (0, 0), (0, A_QK_PAD - A_NOPE)))
    sel = jnp.zeros((A_ROPE, N_HEADS, A_QK_PAD), F32)
    sel = sel.at[:, :, A_NOPE:A_NOPE + A_ROPE].set(jnp.eye(A_ROPE, dtype=F32)[:, None, :])
    wk_t = jnp.concatenate([wk_top, sel], axis=0).reshape(A_KV_LORA + A_ROPE, -1).T.astype(BF16)
    wv_t = kvu[:, :, A_NOPE:].reshape(A_KV_LORA, GROUP_WIDTH).T.astype(BF16)
    d_gain = jnp.concatenate([jnp.broadcast_to(d_q_norm[l], (N_HEADS, HEAD_DIM)),
                              jnp.broadcast_to(d_k_norm[l], (D_KV_HEADS, HEAD_DIM))])[:, :, None]
    groups = N_HEADS // B_KV_HEADS
    sink = jnp.repeat(b_sink[l].reshape(B_KV_HEADS, 1, groups) * LOG2E, BAND_QB, axis=2)
    return dict(w_t=w_t, gq=a_q_norm[l][:, None], gkv=a_kv_norm[l][:, None], wq_t=wq_t, wk_t=wk_t,
                wv_t=wv_t, d_gain=d_gain, sink=sink, w_out=w_out[l].astype(BF16),
                ln_g=ln_g[l][None, :], ln_b=ln_b[l][None, :])


def _to_strided(t, r):
    b, f, s = t.shape
    return t.reshape(b, f, s // r, r).transpose(0, 3, 1, 2).reshape(b * r, f, s // r)


def _from_strided(t, b, r):
    rest = t.shape[1:-1]
    sr = t.shape[-1]
    t = jnp.moveaxis(t.reshape((b, r) + rest + (sr,)), 1, -1)
    return t.reshape((b,) + rest + (sr * r,))


def _pick(n, pref):
    return min(n, pref)


def _layer(x, mod, wts, bias_b, bias_c, ropes):
    b, s, _ = x.shape
    a_c, a_s, d_c, d_s = ropes
    proj = _input_projection(x, mod, wts["w_t"], tm=_pick(s, 1024), tn=1088)

    tp = _pick(s, 512)
    q_a, k_a, v_a = _mla_prep(proj, wts["gq"], wts["gkv"], wts["wq_t"], wts["wk_t"], wts["wv_t"],
                              a_c, a_s, tp)
    y_a = _dense_attention(q_a, k_a, v_a, kv_heads=N_HEADS, dq=A_QK_PAD, k_blk0=0, v_blk0=0,
                           tq=512, nsub=4, tk=512)

    y_b, _ = _banded_attention(proj, proj, proj, bias_b, wts["sink"], kv_heads=B_KV_HEADS,
                               heads_per_step=1, q_row=B_ROW, k_row=B_ROW + GROUP_WIDTH,
                               v_row=B_ROW + GROUP_WIDTH + B_KV_HEADS * HEAD_DIM, tq=_pick(s, 1024))

    outs, lses = [], []
    for (window, r), bias in zip(DILATED_PAIRS, bias_c):
        if r == 1:
            o, lse = _banded_attention(proj, proj, proj, bias, None, kv_heads=N_HEADS, heads_per_step=2,
                                       q_row=C_ROW, k_row=C_ROW + GROUP_WIDTH,
                                       v_row=C_ROW + 2 * GROUP_WIDTH, tq=_pick(s, 1024))
        else:
            qkv = _to_strided(proj[:, C_ROW:, :], r)
            o, lse = _banded_attention(qkv, qkv, qkv, bias, None, kv_heads=N_HEADS, heads_per_step=4,
                                       q_row=0, k_row=GROUP_WIDTH, v_row=2 * GROUP_WIDTH,
                                       tq=_pick(s // r, 1024))
            o, lse = _from_strided(o, b, r), _from_strided(lse, b, r)
        outs.append(o)
        lses.append(lse)
    y_c = _dilated_mix(outs, lses, _pick(s, 1024))

    q_d, k_d = _axial_prep(proj, wts["d_gain"], d_c, d_s, tp)
    y_d = _dense_attention(q_d, k_d, proj, kv_heads=D_KV_HEADS, dq=HEAD_DIM, k_blk0=0,
                           v_blk0=(D_ROW + GROUP_WIDTH + D_KV_HEADS * HEAD_DIM) // HEAD_DIM,
                           tq=512, nsub=2, tk=512)

    return _output_projection((y_a, y_b, y_c, y_d), proj, x, mod, wts["w_out"], wts["ln_g"], wts["ln_b"],
                              tm=_pick(s, 256))


def _trunk(x, mods, layer_wts, rel_bias):
    b, s, _ = x.shape
    ropes = _rope_tables(s)
    bias_b = _band_bias(rel_bias[:, :N_HEADS], B_WINDOW, 1, s, BAND_QB, B_KV_HEADS)
    bias_c = [_band_bias(rel_bias[:, N_HEADS:], window // (2 * r), r, s // r,
                         BAND_QB_WIDE if s // r >= BAND_QB_WIDE + 2 * BAND_HALO else BAND_QB, N_HEADS)
              for window, r in DILATED_PAIRS]
    for l in range(DEPTH):
        x = _layer(x, mods[l], layer_wts[l], bias_b, bias_c, ropes)
    return x


def kernel(x_prompt, x_sample, c_prompt, c_sample, w_ada, b_ada, w_in, a_q_norm, a_q_up, a_kv_norm, a_kv_up, b_sink, d_q_norm, d_k_norm, w_out, ln_g, ln_b, rel_bias):
    bp, bs = c_prompt.shape[0], c_sample.shape[0]
    rows = -(-(bp + bs) // 16) * 16
    c_all = jnp.pad(jnp.concatenate([c_prompt, c_sample], axis=0), ((0, rows - bp - bs), (0, 0)))
    mod = _modulation(c_all, w_ada, b_ada)
    layer_wts = [_layer_weights(l, w_in, a_q_norm, a_q_up, a_kv_norm, a_kv_up, b_sink, d_q_norm,
                                d_k_norm, w_out, ln_g, ln_b) for l in range(DEPTH)]
    mods_p = [mod[l, :bp].reshape(bp, 1, -1) for l in range(DEPTH)]
    mods_s = [mod[l, bp:bp + bs].reshape(bs, 1, -1) for l in range(DEPTH)]
    y_prompt = _trunk(x_prompt, mods_p, layer_wts, rel_bias)
    y_sample = _trunk(x_sample, mods_s, layer_wts, rel_bias)
    return (y_prompt, y_sample)
```

```python
import functools
import math

import numpy as np
import jax
import jax.numpy as jnp
from jax import lax
from jax.experimental import pallas as pl
from jax.experimental.pallas import tpu as pltpu

F32 = jnp.float32
BF16 = jnp.bfloat16

D_MODEL = 2048
DEPTH = 2
HEAD_DIM = 64
N_HEADS = 12
A_Q_LORA = 384
A_KV_LORA = 256
A_NOPE = 64
A_ROPE = 32
A_QK_PAD = 128
B_KV_HEADS = 4
B_WINDOW = 128
DILATED_PAIRS = ((128, 1), (512, 4), (2048, 16))
D_KV_HEADS = 4
GROUP_WIDTH = N_HEADS * HEAD_DIM
D_MIX = 4 * GROUP_WIDTH
A_COLS = A_Q_LORA + A_KV_LORA + A_ROPE
B_COLS = (N_HEADS + 2 * B_KV_HEADS) * HEAD_DIM
C_COLS = 3 * N_HEADS * HEAD_DIM
D_COLS = (N_HEADS + 2 * D_KV_HEADS) * HEAD_DIM
GRID_W = 64
NUM_BUCKETS = 32
T5_MAX_DISTANCE = 1024
ROPE_THETA = 10000.0
EPS = 1e-6
NEG = -1e30
ALPHA = (2 * DEPTH) ** 0.25
LOG2E = 1.4426950408889634

Z_ROW = 0
A_ROW = D_MIX
A_ROWS = 768
B_ROW = A_ROW + A_ROWS
D_ROW = B_ROW + B_COLS
C_ROW = D_ROW + D_COLS
N_PROJ = C_ROW + C_COLS

LANE = 128
BAND_QB = 128
BAND_HALO = 128
BAND_UNITS = 16
BAND_AHEAD = 3
BAND_TQ = 2048
BAND_STEP_ELEMS = 6 * 2 ** 20
VMEM_LIMIT = 52 * 2 ** 20


def _cparams(sem):
    return pltpu.CompilerParams(dimension_semantics=sem, vmem_limit_bytes=VMEM_LIMIT)


def _silu(x):
    return x / (1.0 + jnp.exp(-x))


def _tn_dot(a, b):
    return lax.dot_general(a, b, (((0,), (0,)), ((), ())), preferred_element_type=F32)


def _nt_dot(a, b):
    return lax.dot_general(a, b, (((1,), (1,)), ((), ())), preferred_element_type=F32)


def _mod_kernel(c_ref, w_ref, b_ref, o_ref):
    sc = _silu(c_ref[...]).astype(BF16)
    o_ref[0] = jnp.dot(sc, w_ref[0].astype(BF16), preferred_element_type=F32) + b_ref[0]


def _modulation(c_all, w_ada, b_ada):
    depth, _, n = w_ada.shape
    rows = c_all.shape[0]
    tn = 768
    return pl.pallas_call(
        _mod_kernel,
        grid=(depth, n // tn),
        in_specs=[pl.BlockSpec((rows, D_MODEL), lambda l, j: (0, 0)),
                  pl.BlockSpec((1, D_MODEL, tn), lambda l, j: (l, 0, j)),
                  pl.BlockSpec((1, 1, tn), lambda l, j: (l, 0, j))],
        out_specs=pl.BlockSpec((1, rows, tn), lambda l, j: (l, 0, j)),
        out_shape=jax.ShapeDtypeStruct((depth, rows, n), F32),
        compiler_params=_cparams(("parallel", "parallel")),
        name="adaln_modulation",
    )(c_all, w_ada, b_ada.reshape(depth, 1, n))


def _inproj_kernel(x_ref, sh_ref, sc_ref, w_ref, o_ref, h_ref):
    @pl.when(pl.program_id(2) == 0)
    def _():
        x = x_ref[0]
        mu = jnp.mean(x, axis=-1, keepdims=True)
        xc = x - mu
        var = jnp.mean(xc * xc, axis=-1, keepdims=True)
        h = (xc * lax.rsqrt(var + EPS)) * (1.0 + sc_ref[0]) + sh_ref[0]
        h_ref[...] = h.astype(BF16)

    o_ref[0] = _nt_dot(w_ref[...], h_ref[...]).astype(BF16)


def _input_projection(x, mod, w_t, tm, tn):
    b, s, _ = x.shape
    n = w_t.shape[0]
    return pl.pallas_call(
        _inproj_kernel,
        grid=(b, s // tm, n // tn),
        in_specs=[pl.BlockSpec((1, tm, D_MODEL), lambda i, t, j: (i, t, 0)),
                  pl.BlockSpec((1, 1, D_MODEL), lambda i, t, j: (i, 0, 0)),
                  pl.BlockSpec((1, 1, D_MODEL), lambda i, t, j: (i, 0, 1)),
                  pl.BlockSpec((tn, D_MODEL), lambda i, t, j: (j, 0))],
        out_specs=pl.BlockSpec((1, tn, tm), lambda i, t, j: (i, j, t)),
        out_shape=jax.ShapeDtypeStruct((b, n, s), BF16),
        scratch_shapes=[pltpu.VMEM((tm, D_MODEL), BF16)],
        compiler_params=_cparams(("parallel", "parallel", "arbitrary")),
        name="input_projection",
    )(x, mod, mod, w_t)


def _rms_rows(x, g):
    return x * lax.rsqrt(jnp.mean(x * x, axis=0, keepdims=True) + EPS) * g


def _mla_prep_kernel(a_ref, gq_ref, gkv_ref, wq_ref, wk_ref, wv_ref, rc_ref, rs_ref,
                     q_ref, k_ref, v_ref, *, q_scale):
    a = a_ref[0].astype(F32)
    tm = a.shape[1]
    qn = _rms_rows(a[0:A_Q_LORA], gq_ref[...]).astype(BF16)
    kvn = _rms_rows(a[A_Q_LORA:A_Q_LORA + A_KV_LORA], gkv_ref[...])
    kpe = a[A_Q_LORA + A_KV_LORA:A_COLS]
    rc = rc_ref[...]
    rs = rs_ref[...]
    half = A_ROPE // 2
    lo, mid, hi = A_NOPE, A_NOPE + half, A_NOPE + A_ROPE

    q = jnp.dot(wq_ref[...], qn, preferred_element_type=F32)
    q = q.reshape(N_HEADS, A_QK_PAD, tm)
    q_sw = jnp.concatenate([q[:, :lo], q[:, mid:hi], q[:, lo:mid], q[:, hi:]], axis=1)
    q = (q * rc[None] + q_sw * rs[None]) * q_scale
    q_ref[0] = q.reshape(N_HEADS * A_QK_PAD, tm).astype(BF16)

    kpe_sw = jnp.concatenate([kpe[half:], kpe[:half]], axis=0)
    kpe = kpe * rc[lo:hi] + kpe_sw * rs[lo:hi]
    xk = jnp.concatenate([kvn, kpe], axis=0).astype(BF16)
    k_ref[0] = jnp.dot(wk_ref[...], xk, preferred_element_type=F32).astype(BF16)
    v_ref[0] = jnp.dot(wv_ref[...], xk[:A_KV_LORA], preferred_element_type=F32).astype(BF16)


def _mla_prep(proj, gq, gkv, wq_t, wk_t, wv_t, rope_c, rope_s, tm):
    b, _, s = proj.shape
    hq = N_HEADS * A_QK_PAD
    full = lambda shape: pl.BlockSpec(shape, lambda i, t: (0,) * len(shape))
    return pl.pallas_call(
        functools.partial(_mla_prep_kernel, q_scale=(A_NOPE + A_ROPE) ** -0.5 * LOG2E),
        grid=(b, s // tm),
        in_specs=[pl.BlockSpec((1, A_ROWS, tm), lambda i, t: (i, A_ROW // A_ROWS, t)),
                  full((A_Q_LORA, 1)), full((A_KV_LORA, 1)),
                  full(wq_t.shape), full(wk_t.shape), full(wv_t.shape),
                  pl.BlockSpec((A_QK_PAD, tm), lambda i, t: (0, t)),
                  pl.BlockSpec((A_QK_PAD, tm), lambda i, t: (0, t))],
        out_specs=[pl.BlockSpec((1, hq, tm), lambda i, t: (i, 0, t)),
                   pl.BlockSpec((1, hq, tm), lambda i, t: (i, 0, t)),
                   pl.BlockSpec((1, GROUP_WIDTH, tm), lambda i, t: (i, 0, t))],
        out_shape=[jax.ShapeDtypeStruct((b, hq, s), BF16),
                   jax.ShapeDtypeStruct((b, hq, s), BF16),
                   jax.ShapeDtypeStruct((b, GROUP_WIDTH, s), BF16)],
        compiler_params=_cparams(("parallel", "parallel")),
        name="latent_attention_prep",
    )(proj, gq, gkv, wq_t, wk_t, wv_t, rope_c, rope_s)


def _axial_prep_kernel(x_ref, g_ref, rc_ref, rs_ref, q_ref, k_ref, *, q_scale):
    x = x_ref[0].astype(F32)
    tm = x.shape[1]
    nh = N_HEADS + D_KV_HEADS
    x = x.reshape(nh, HEAD_DIM, tm)
    x = x * lax.rsqrt(jnp.mean(x * x, axis=1, keepdims=True) + EPS) * g_ref[...]
    q4 = HEAD_DIM // 4
    x_sw = jnp.concatenate([x[:, q4:2 * q4], x[:, :q4], x[:, 3 * q4:], x[:, 2 * q4:3 * q4]], axis=1)
    x = x * rc_ref[...][None] + x_sw * rs_ref[...][None]
    q_ref[0] = (x[:N_HEADS] * q_scale).reshape(GROUP_WIDTH, tm).astype(BF16)
    k_ref[0] = x[N_HEADS:].reshape(D_KV_HEADS * HEAD_DIM, tm).astype(BF16)


def _axial_prep(proj, gains, rope_c, rope_s, tm):
    b, _, s = proj.shape
    rows = GROUP_WIDTH + D_KV_HEADS * HEAD_DIM
    return pl.pallas_call(
        functools.partial(_axial_prep_kernel, q_scale=HEAD_DIM ** -0.5 * LOG2E),
        grid=(b, s // tm),
        in_specs=[pl.BlockSpec((1, rows, tm), lambda i, t: (i, D_ROW // rows, t)),
                  pl.BlockSpec((N_HEADS + D_KV_HEADS, HEAD_DIM, 1), lambda i, t: (0, 0, 0)),
                  pl.BlockSpec((HEAD_DIM, tm), lambda i, t: (0, t)),
                  pl.BlockSpec((HEAD_DIM, tm), lambda i, t: (0, t))],
        out_specs=[pl.BlockSpec((1, GROUP_WIDTH, tm), lambda i, t: (i, 0, t)),
                   pl.BlockSpec((1, D_KV_HEADS * HEAD_DIM, tm), lambda i, t: (i, 0, t))],
        out_shape=[jax.ShapeDtypeStruct((b, GROUP_WIDTH, s), BF16),
                   jax.ShapeDtypeStruct((b, D_KV_HEADS * HEAD_DIM, s), BF16)],
        compiler_params=_cparams(("parallel", "parallel")),
        name="axial_attention_prep",
    )(proj, gains, rope_c, rope_s)


V_EXT = HEAD_DIM + 16
FLASH_CHUNKS = 4


def _flash_kernel(q_ref, k_ref, v_ref, o_ref, s_ref, *, groups, nsub, dq, tk):
    tq = q_ref.shape[2] // nsub
    nk = k_ref.shape[2] // tk
    chains = [(g, u) for g in range(groups) for u in range(nsub)]
    ones = jnp.ones((V_EXT - HEAD_DIM, tk), BF16)

    def scores(c, ci, slot):
        g, u = chains[ci]
        off = pl.multiple_of(c * tk, tk)
        s = _tn_dot(k_ref[0, :, pl.ds(off, tk)], q_ref[0, g * dq:(g + 1) * dq, u * tq:(u + 1) * tq])
        s_ref[ci, slot, :, :tq] = s
        return jnp.max(s, axis=0, keepdims=True)

    def consume(c, ci, slot, cmax, m, acc):
        off = pl.multiple_of(c * tk, tk)
        v_ext = jnp.concatenate([v_ref[0, :, pl.ds(off, tk)], ones], axis=0)
        m_new = jnp.maximum(m, cmax)
        alpha = jnp.exp2(m - m_new)
        p = jnp.exp2(s_ref[ci, slot, :, :tq] - m_new).astype(BF16)
        acc = alpha * acc + jnp.dot(v_ext, p, preferred_element_type=F32)
        return m_new, acc

    def chunk_group(c0, carry, last):
        state = list(carry)
        for t in range(FLASH_CHUNKS):
            for ci, (cmax, m, acc) in enumerate(state):
                cmax_next = cmax
                if not (last and t == FLASH_CHUNKS - 1):
                    cmax_next = scores(c0 + t + 1, ci, (t + 1) % 2)
                m, acc = consume(c0 + t, ci, t % 2, cmax, m, acc)
                state[ci] = (cmax_next, m, acc)
        return tuple(state)

    init = tuple((scores(0, ci, 0), jnp.full((1, tq), NEG, F32), jnp.zeros((V_EXT, tq), F32))
                 for ci in range(len(chains)))
    carry = lax.fori_loop(0, nk // FLASH_CHUNKS - 1,
                          lambda i, carry: chunk_group(i * FLASH_CHUNKS, carry, False), init)
    carry = chunk_group(nk - FLASH_CHUNKS, carry, True)
    for (g, u), (_, _, acc) in zip(chains, carry):
        o_ref[0, g * HEAD_DIM:(g + 1) * HEAD_DIM, u * tq:(u + 1) * tq] = (
            acc[:HEAD_DIM] / acc[HEAD_DIM:HEAD_DIM + 1]).astype(BF16)


def _dense_attention(q, k, v, *, kv_heads, dq, k_blk0, v_blk0, tq, nsub, tk):
    b, _, s = q.shape
    groups = N_HEADS // kv_heads
    assert FLASH_CHUNKS % 2 == 0 and (s // tk) % FLASH_CHUNKS == 0 and s % (tq * nsub) == 0
    return pl.pallas_call(
        functools.partial(_flash_kernel, groups=groups, nsub=nsub, dq=dq, tk=tk),
        grid=(b, kv_heads, s // (tq * nsub)),
        in_specs=[pl.BlockSpec((1, groups * dq, tq * nsub), lambda i, h, t: (i, h, t)),
                  pl.BlockSpec((1, dq, s), lambda i, h, t: (i, k_blk0 + h, 0)),
                  pl.BlockSpec((1, HEAD_DIM, s), lambda i, h, t: (i, v_blk0 + h, 0))],
        out_specs=pl.BlockSpec((1, groups * HEAD_DIM, tq * nsub), lambda i, h, t: (i, h, t)),
        out_shape=jax.ShapeDtypeStruct((b, GROUP_WIDTH, s), BF16),
        scratch_shapes=[pltpu.VMEM((groups * nsub, 2, tk, tq + LANE), F32)],
        compiler_params=_cparams(("parallel", "parallel", "arbitrary")),
        name="dense_attention",
    )(q, k, v)


def _t5_bucket(rel):
    half = NUM_BUCKETS // 2
    exact = half // 2
    n = jnp.abs(rel)
    large = exact + (jnp.log(jnp.maximum(n, 1).astype(F32) / exact)
                     / math.log(T5_MAX_DISTANCE / exact) * (half - exact)).astype(jnp.int32)
    large = jnp.minimum(large, half - 1)
    return jnp.where(rel > 0, half, 0) + jnp.where(n < exact, n, large)


def _band_buckets(w, dilation, kw, qb):
    kk = jnp.arange(kw)[:, None]
    qq = jnp.arange(qb)[None, :]
    out = []
    for shift in (0, BAND_HALO, kw - qb):
        rel = kk - shift - qq
        out.append(jnp.where(jnp.abs(rel) <= w, _t5_bucket(dilation * rel), -1))
    return jnp.stack(out).astype(jnp.int32)


def _bias_kernel(bucket_ref, tab_ref, o_ref, *, groups):
    h = pl.program_id(0)
    bk = bucket_ref[...]
    qb = bk.shape[2]
    for g in range(groups):
        acc = jnp.zeros(bk.shape, F32)
        for b in range(NUM_BUCKETS):
            acc = jnp.where(bk == b, tab_ref[b, h * groups + g], acc)
        o_ref[:, 0, :, g * qb:(g + 1) * qb] = jnp.where(bk < 0, NEG, acc * LOG2E)


def _band_bias(table, w, dilation, seq, qb, kv_heads):
    groups = N_HEADS // kv_heads
    kw = min(seq, qb + 2 * BAND_HALO)
    buckets = _band_buckets(w, dilation, kw, qb)
    return pl.pallas_call(
        functools.partial(_bias_kernel, groups=groups),
        grid=(kv_heads,),
        in_specs=[pl.BlockSpec((3, kw, qb), lambda h: (0, 0, 0)),
                  pl.BlockSpec(memory_space=pltpu.SMEM)],
        out_specs=pl.BlockSpec((3, 1, kw, groups * qb), lambda h: (0, h, 0, 0)),
        out_shape=jax.ShapeDtypeStruct((3, kv_heads, kw, groups * qb), F32),
        compiler_params=_cparams(("parallel",)),
        name="relative_bias_tables",
    )(buckets, table)


def _banded_kernel(q_ref, k_ref, v_ref, tab_ref, *rest, groups, heads, kw, has_sink):
    if has_sink:
        sink_ref, o_ref, lse_ref = rest
    else:
        o_ref, lse_ref = rest
    nb, _, tq = q_ref.shape
    seq = k_ref.shape[2]
    t0 = pl.program_id(2) * tq
    qb = tab_ref.shape[3] // groups

    def window(j):
        qoff = pl.multiple_of(j * qb, qb)
        q0 = t0 + qoff
        sel = jnp.where(q0 == 0, 0, jnp.where(q0 == seq - qb, 2, 1))
        ks = pl.multiple_of(jnp.clip(q0 - BAND_HALO, 0, seq - kw), BAND_HALO)
        return qoff, sel, ks

    def scores(bb, j, hb):
        qoff, sel, ks = window(j)
        k_t = k_ref[bb, hb * HEAD_DIM:(hb + 1) * HEAD_DIM, pl.ds(ks, kw)]
        r0 = hb * groups * HEAD_DIM
        q_cat = jnp.concatenate(
            [q_ref[bb, r0 + g * HEAD_DIM:r0 + (g + 1) * HEAD_DIM, pl.ds(qoff, qb)]
             for g in range(groups)], axis=1)
        return _tn_dot(k_t, q_cat) + tab_ref[sel, hb]

    ones = jnp.ones((V_EXT - HEAD_DIM, kw), BF16)

    def finish(bb, j, hb, s):
        qoff, _, ks = window(j)
        v_ext = jnp.concatenate([v_ref[bb, hb * HEAD_DIM:(hb + 1) * HEAD_DIM, pl.ds(ks, kw)], ones], axis=0)
        m = jnp.max(s, axis=0, keepdims=True)
        if has_sink:
            sink = sink_ref[hb]
            m = jnp.maximum(m, sink)
        p = jnp.exp2(s - m).astype(BF16)
        ol = jnp.dot(v_ext, p, preferred_element_type=F32)
        l = ol[HEAD_DIM:HEAD_DIM + 1]
        if has_sink:
            l = l + jnp.exp2(sink - m)
        return qoff, (ol[:HEAD_DIM] / l).astype(o_ref.dtype), m + jnp.log2(l)

    nsub = tq // qb
    per_body = max(1, min(nsub, BAND_UNITS // heads))
    assert nsub % per_body == 0
    ngroups = nsub // per_body

    def body(idx, carry):
        bb = idx // ngroups
        jj = idx % ngroups
        units = [(bb, jj * per_body + u, hb) for u in range(per_body) for hb in range(heads)]
        s_vals = [scores(*unit) for unit in units[:BAND_AHEAD]]
        done = []
        for i, unit in enumerate(units):
            if i + BAND_AHEAD < len(units):
                s_vals.append(scores(*units[i + BAND_AHEAD]))
            done.append(finish(*unit, s_vals[i]))
        for (_, _, hb), (qoff, o, lse) in zip(units, done):
            r0 = hb * groups * HEAD_DIM
            for g in range(groups):
                o_ref[bb, r0 + g * HEAD_DIM:r0 + (g + 1) * HEAD_DIM, pl.ds(qoff, qb)] = o[:, g * qb:(g + 1) * qb]
                lse_ref[bb, hb, g:g + 1, pl.ds(qoff, qb)] = lse[:, g * qb:(g + 1) * qb]
        return carry

    lax.fori_loop(0, nb * ngroups, body, 0)


def _largest_divisor(n, cap, ok=lambda d: True):
    return max(d for d in range(1, n + 1) if n % d == 0 and d <= max(cap, 1) and ok(d))


def _banded_attention(q, k, v, bias, sink, *, kv_heads, q_row, k_row, v_row):
    b, _, seq = q.shape
    groups = N_HEADS // kv_heads
    kw = bias.shape[2]
    tq = min(seq, BAND_TQ)
    def aligned(d):
        return q_row % (d * groups * HEAD_DIM) == 0 and k_row % (d * HEAD_DIM) == 0 and v_row % (d * HEAD_DIM) == 0

    per_head = HEAD_DIM * (2 * groups * tq + 2 * seq)
    hb = _largest_divisor(kv_heads, BAND_STEP_ELEMS // per_head, aligned)
    nb = _largest_divisor(b, BAND_STEP_ELEMS // (per_head * hb))
    qrows, krows = hb * groups * HEAD_DIM, hb * HEAD_DIM
    in_specs = [pl.BlockSpec((nb, qrows, tq), lambda i, h, t: (i, q_row // qrows + h, t)),
                pl.BlockSpec((nb, krows, seq), lambda i, h, t: (i, k_row // krows + h, 0)),
                pl.BlockSpec((nb, krows, seq), lambda i, h, t: (i, v_row // krows + h, 0)),
                pl.BlockSpec((3, hb, kw, bias.shape[3]), lambda i, h, t: (0, h, 0, 0))]
    args = [q, k, v, bias]
    if sink is not None:
        in_specs.append(pl.BlockSpec((hb, 1, bias.shape[3]), lambda i, h, t: (h, 0, 0)))
        args.append(sink)
    return pl.pallas_call(
        functools.partial(_banded_kernel, groups=groups, heads=hb, kw=kw, has_sink=sink is not None),
        grid=(b // nb, kv_heads // hb, seq // tq),
        in_specs=in_specs,
        out_specs=[pl.BlockSpec((nb, qrows, tq), lambda i, h, t: (i, h, t)),
                   pl.BlockSpec((nb, hb, groups, tq), lambda i, h, t: (i, h, 0, t))],
        out_shape=[jax.ShapeDtypeStruct((b, GROUP_WIDTH, seq), BF16),
                   jax.ShapeDtypeStruct((b, kv_heads, groups, seq), F32)],
        compiler_params=_cparams(("parallel", "parallel", "arbitrary")),
        name="banded_attention",
    )(*args)


def _mix_kernel(o1_ref, o2_ref, o3_ref, l1_ref, l2_ref, l3_ref, y_ref):
    for h in range(N_HEADS):
        a1, a2, a3 = l1_ref[0, h], l2_ref[0, h], l3_ref[0, h]
        mx = jnp.maximum(jnp.maximum(a1, a2), a3)
        e1, e2, e3 = jnp.exp2(a1 - mx), jnp.exp2(a2 - mx), jnp.exp2(a3 - mx)
        inv = 1.0 / (e1 + e2 + e3)
        rows = slice(h * HEAD_DIM, (h + 1) * HEAD_DIM)
        y = ((e1 * inv) * o1_ref[0, rows, :].astype(F32) + (e2 * inv) * o2_ref[0, rows, :].astype(F32)
             + (e3 * inv) * o3_ref[0, rows, :].astype(F32))
        y_ref[0, rows, :] = y.astype(BF16)


def _dilated_mix(outs, lses, tm):
    b, _, s = outs[0].shape
    ospec = pl.BlockSpec((1, GROUP_WIDTH, tm), lambda i, t: (i, 0, t))
    lspec = pl.BlockSpec((1, N_HEADS, 1, tm), lambda i, t: (i, 0, 0, t))
    return pl.pallas_call(
        _mix_kernel,
        grid=(b, s // tm),
        in_specs=[ospec] * 3 + [lspec] * 3,
        out_specs=ospec,
        out_shape=jax.ShapeDtypeStruct((b, GROUP_WIDTH, s), BF16),
        compiler_params=_cparams(("parallel", "parallel")),
        name="dilated_mixture",
    )(*outs, *lses)


def _outproj_kernel(ya_ref, yb_ref, yc_ref, yd_ref, z_ref, x_ref, gate_ref, w_ref, g_ref, b_ref, o_ref):
    acc = None
    for i, y_ref in enumerate((ya_ref, yb_ref, yc_ref, yd_ref)):
        rows = slice(i * GROUP_WIDTH, (i + 1) * GROUP_WIDTH)
        z = z_ref[0, rows, :].astype(F32)
        gated = (y_ref[0].astype(F32) * _silu(z)).astype(BF16)
        part = _tn_dot(gated, w_ref[rows, :])
        acc = part if acc is None else acc + part
    r = ALPHA * x_ref[0] + gate_ref[0] * acc
    mu = jnp.mean(r, axis=-1, keepdims=True)
    rc = r - mu
    var = jnp.mean(rc * rc, axis=-1, keepdims=True)
    o_ref[0] = (rc * lax.rsqrt(var + EPS)) * g_ref[...] + b_ref[...]


def _output_projection(ys, proj, x, mod, w_out, ln_g, ln_b, tm):
    b, s, _ = x.shape
    yspec = pl.BlockSpec((1, GROUP_WIDTH, tm), lambda i, t: (i, 0, t))
    return pl.pallas_call(
        _outproj_kernel,
        grid=(b, s // tm),
        in_specs=[yspec] * 4 + [
            pl.BlockSpec((1, D_MIX, tm), lambda i, t: (i, Z_ROW // D_MIX, t)),
            pl.BlockSpec((1, tm, D_MODEL), lambda i, t: (i, t, 0)),
            pl.BlockSpec((1, 1, D_MODEL), lambda i, t: (i, 0, 2)),
            pl.BlockSpec((D_MIX, D_MODEL), lambda i, t: (0, 0)),
            pl.BlockSpec((1, D_MODEL), lambda i, t: (0, 0)),
            pl.BlockSpec((1, D_MODEL), lambda i, t: (0, 0))],
        out_specs=pl.BlockSpec((1, tm, D_MODEL), lambda i, t: (i, t, 0)),
        out_shape=jax.ShapeDtypeStruct((b, s, D_MODEL), F32),
        compiler_params=_cparams(("parallel", "parallel")),
        name="output_projection",
    )(*ys, proj, x, mod, w_out, ln_g, ln_b)


def _rope_tables(seq):
    pos = jnp.arange(seq)
    inv = ROPE_THETA ** (-jnp.arange(0, A_ROPE, 2, dtype=F32) / A_ROPE)

    def cs(p):
        ang = p.astype(F32)[:, None] * inv[None, :]
        return jnp.cos(ang).T, jnp.sin(ang).T

    c, sn = cs(pos)
    ones, zeros = jnp.ones((A_NOPE, seq), F32), jnp.zeros((A_NOPE, seq), F32)
    pad = jnp.zeros((A_QK_PAD - A_NOPE - A_ROPE, seq), F32)
    a_c = jnp.concatenate([ones, c, c, pad], axis=0)
    a_s = jnp.concatenate([zeros, -sn, sn, pad], axis=0)
    cr, sr = cs(pos // GRID_W)
    cc, sc = cs(pos % GRID_W)
    d_c = jnp.concatenate([cr, cr, cc, cc], axis=0)
    d_s = jnp.concatenate([-sr, sr, -sc, sc], axis=0)
    return a_c, a_s, d_c, d_s


def _layer_weights(l, w_in, a_q_norm, a_q_up, a_kv_norm, a_kv_up, b_sink, d_q_norm, d_k_norm, w_out,
                   ln_g, ln_b):
    wa, wb, wc, wd, wz = jnp.split(w_in[l], np.cumsum([A_COLS, B_COLS, C_COLS, D_COLS]).tolist(), axis=1)
    wa = jnp.pad(wa, ((0, 0), (0, A_ROWS - A_COLS)))
    band_scale = jnp.concatenate([jnp.full((GROUP_WIDTH,), HEAD_DIM ** -0.5 * LOG2E, F32),
                                  jnp.ones((B_COLS - GROUP_WIDTH,), F32)])
    wb = wb * band_scale[None, :]
    wc = wc * jnp.pad(band_scale[:GROUP_WIDTH], (0, C_COLS - GROUP_WIDTH), constant_values=1.0)[None, :]
    w_t = jnp.concatenate([wz, wa, wb, wd, wc], axis=1).T.astype(BF16)
    dqk = A_NOPE + A_ROPE
    wq = jnp.pad(a_q_up[l].reshape(A_Q_LORA, N_HEADS, dqk), ((0, 0), (0, 0), (0, A_QK_PAD - dqk)))
    wq_t = wq.reshape(A_Q_LORA, N_HEADS * A_QK_PAD).T.astype(BF16)
    kvu = a_kv_up[l].reshape(A_KV_LORA, N_HEADS, A_NOPE + HEAD_DIM)
    wk_top = jnp.pad(kvu[:, :, :A_NOPE], ((0, 0), (0, 0), (0, A_QK_PAD - A_NOPE)))
    sel = jnp.zeros((A_ROPE, N_HEADS, A_QK_PAD), F32)
    sel = sel.at[:, :, A_NOPE:A_NOPE + A_ROPE].set(jnp.eye(A_ROPE, dtype=F32)[:, None, :])
    wk_t = jnp.concatenate([wk_top, sel], axis=0).reshape(A_KV_LORA + A_ROPE, -1).T.astype(BF16)
    wv_t = kvu[:, :, A_NOPE:].reshape(A_KV_LORA, GROUP_WIDTH).T.astype(BF16)
    d_gain = jnp.concatenate([jnp.broadcast_to(d_q_norm[l], (N_HEADS, HEAD_DIM)),
                              jnp.broadcast_to(d_k_norm[l], (D_KV_HEADS, HEAD_DIM))])[:, :, None]
    groups = N_HEADS // B_KV_HEADS
    sink = jnp.repeat(b_sink[l].reshape(B_KV_HEADS, 1, groups) * LOG2E, BAND_QB, axis=2)
    return dict(w_t=w_t, gq=a_q_norm[l][:, None], gkv=a_kv_norm[l][:, None], wq_t=wq_t, wk_t=wk_t,
                wv_t=wv_t, d_gain=d_gain, sink=sink, w_out=w_out[l].astype(BF16),
                ln_g=ln_g[l][None, :], ln_b=ln_b[l][None, :])


def _to_strided(t, r):
    b, f, s = t.shape
    return t.reshape(b, f, s // r, r).transpose(0, 3, 1, 2).reshape(b * r, f, s // r)


def _from_strided(t, b, r):
    rest = t.shape[1:-1]
    sr = t.shape[-1]
    t = jnp.moveaxis(t.reshape((b, r) + rest + (sr,)), 1, -1)
    return t.reshape((b,) + rest + (sr * r,))


def _pick(n, pref):
    return min(n, pref)


def _layer(x, mod, wts, bias_b, bias_c, ropes):
    b, s, _ = x.shape
    a_c, a_s, d_c, d_s = ropes
    proj = _input_projection(x, mod, wts["w_t"], tm=_pick(s, 1024), tn=1088)

    tp = _pick(s, 512)
    q_a, k_a, v_a = _mla_prep(proj, wts["gq"], wts["gkv"], wts["wq_t"], wts["wk_t"], wts["wv_t"],
                              a_c, a_s, tp)
    y_a = _dense_attention(q_a, k_a, v_a, kv_heads=N_HEADS, dq=A_QK_PAD, k_blk0=0, v_blk0=0,
                           tq=512, nsub=4, tk=512)

    y_b, _ = _banded_attention(proj, proj, proj, bias_b, wts["sink"], kv_heads=B_KV_HEADS,
                               q_row=B_ROW, k_row=B_ROW + GROUP_WIDTH,
                               v_row=B_ROW + GROUP_WIDTH + B_KV_HEADS * HEAD_DIM)

    outs, lses = [], []
    for (window, r), bias in zip(DILATED_PAIRS, bias_c):
        if r == 1:
            o, lse = _banded_attention(proj, proj, proj, bias, None, kv_heads=N_HEADS,
                                       q_row=C_ROW, k_row=C_ROW + GROUP_WIDTH, v_row=C_ROW + 2 * GROUP_WIDTH)
        else:
            qkv = _to_strided(proj[:, C_ROW:, :], r)
            o, lse = _banded_attention(qkv, qkv, qkv, bias, None, kv_heads=N_HEADS,
                                       q_row=0, k_row=GROUP_WIDTH, v_row=2 * GROUP_WIDTH)
            o, lse = _from_strided(o, b, r), _from_strided(lse, b, r)
        outs.append(o)
        lses.append(lse)
    y_c = _dilated_mix(outs, lses, _pick(s, 1024))

    q_d, k_d = _axial_prep(proj, wts["d_gain"], d_c, d_s, tp)
    y_d = _dense_attention(q_d, k_d, proj, kv_heads=D_KV_HEADS, dq=HEAD_DIM, k_blk0=0,
                           v_blk0=(D_ROW + GROUP_WIDTH + D_KV_HEADS * HEAD_DIM) // HEAD_DIM,
                           tq=512, nsub=2, tk=512)

    return _output_projection((y_a, y_b, y_c, y_d), proj, x, mod, wts["w_out"], wts["ln_g"], wts["ln_b"],
                              tm=_pick(s, 256))


def _trunk(x, mods, layer_wts, rel_bias):
    b, s, _ = x.shape
    ropes = _rope_tables(s)
    bias_b = _band_bias(rel_bias[:, :N_HEADS], B_WINDOW, 1, s, BAND_QB, B_KV_HEADS)
    bias_c = [_band_bias(rel_bias[:, N_HEADS:], window // (2 * r), r, s // r, BAND_QB, N_HEADS)
              for window, r in DILATED_PAIRS]
    for l in range(DEPTH):
        x = _layer(x, mods[l], layer_wts[l], bias_b, bias_c, ropes)
    return x


def kernel(x_prompt, x_sample, c_prompt, c_sample, w_ada, b_ada, w_in, a_q_norm, a_q_up, a_kv_norm, a_kv_up, b_sink, d_q_norm, d_k_norm, w_out, ln_g, ln_b, rel_bias):
    bp, bs = c_prompt.shape[0], c_sample.shape[0]
    rows = -(-(bp + bs) // 16) * 16
    c_all = jnp.pad(jnp.concatenate([c_prompt, c_sample], axis=0), ((0, rows - bp - bs), (0, 0)))
    mod = _modulation(c_all, w_ada, b_ada)
    layer_wts = [_layer_weights(l, w_in, a_q_norm, a_q_up, a_kv_norm, a_kv_up, b_sink, d_q_norm,
                                d_k_norm, w_out, ln_g, ln_b) for l in range(DEPTH)]
    mods_p = [mod[l, :bp].reshape(bp, 1, -1) for l in range(DEPTH)]
    mods_s = [mod[l, bp:bp + bs].reshape(bs, 1, -1) for l in range(DEPTH)]
    y_prompt = _trunk(x_prompt, mods_p, layer_wts, rel_bias)
    y_sample = _trunk(x_sample, mods_s, layer_wts, rel_bias)
    return (y_prompt, y_sample)
```

```python
import functools
import math

import numpy as np
import jax
import jax.numpy as jnp
from jax import lax
from jax.experimental import pallas as pl
from jax.experimental.pallas import tpu as pltpu

F32 = jnp.float32
BF16 = jnp.bfloat16

D_MODEL = 2048
DEPTH = 2
HEAD_DIM = 64
N_HEADS = 12
A_Q_LORA = 384
A_KV_LORA = 256
A_NOPE = 64
A_ROPE = 32
A_QK_PAD = 128
B_KV_HEADS = 4
B_WINDOW = 128
DILATED_PAIRS = ((128, 1), (512, 4), (2048, 16))
D_KV_HEADS = 4
GROUP_WIDTH = N_HEADS * HEAD_DIM
D_MIX = 4 * GROUP_WIDTH
A_COLS = A_Q_LORA + A_KV_LORA + A_ROPE
B_COLS = (N_HEADS + 2 * B_KV_HEADS) * HEAD_DIM
C_COLS = 3 * N_HEADS * HEAD_DIM
D_COLS = (N_HEADS + 2 * D_KV_HEADS) * HEAD_DIM
GRID_W = 64
NUM_BUCKETS = 32
T5_MAX_DISTANCE = 1024
ROPE_THETA = 10000.0
EPS = 1e-6
NEG = -1e30
ALPHA = (2 * DEPTH) ** 0.25
LOG2E = 1.4426950408889634

Z_ROW = 0
A_ROW = D_MIX
A_ROWS = 768
B_ROW = A_ROW + A_ROWS
D_ROW = B_ROW + B_COLS
C_ROW = D_ROW + D_COLS
N_PROJ = C_ROW + C_COLS

LANE = 128
BAND_QB = 128
BAND_HALO = 128
BAND_UNITS = 16
BAND_AHEAD = 3
BAND_TQ = 2048
BAND_STEP_ELEMS = 6 * 2 ** 20
VMEM_LIMIT = 52 * 2 ** 20


def _cparams(sem):
    return pltpu.CompilerParams(dimension_semantics=sem, vmem_limit_bytes=VMEM_LIMIT)


def _silu(x):
    return x / (1.0 + jnp.exp(-x))


def _tn_dot(a, b):
    return lax.dot_general(a, b, (((0,), (0,)), ((), ())), preferred_element_type=F32)


def _nt_dot(a, b):
    return lax.dot_general(a, b, (((1,), (1,)), ((), ())), preferred_element_type=F32)


def _mod_kernel(c_ref, w_ref, b_ref, o_ref):
    sc = _silu(c_ref[...]).astype(BF16)
    o_ref[0] = jnp.dot(sc, w_ref[0].astype(BF16), preferred_element_type=F32) + b_ref[0]


def _modulation(c_all, w_ada, b_ada):
    depth, _, n = w_ada.shape
    rows = c_all.shape[0]
    tn = 768
    return pl.pallas_call(
        _mod_kernel,
        grid=(depth, n // tn),
        in_specs=[pl.BlockSpec((rows, D_MODEL), lambda l, j: (0, 0)),
                  pl.BlockSpec((1, D_MODEL, tn), lambda l, j: (l, 0, j)),
                  pl.BlockSpec((1, 1, tn), lambda l, j: (l, 0, j))],
        out_specs=pl.BlockSpec((1, rows, tn), lambda l, j: (l, 0, j)),
        out_shape=jax.ShapeDtypeStruct((depth, rows, n), F32),
        compiler_params=_cparams(("parallel", "parallel")),
        name="adaln_modulation",
    )(c_all, w_ada, b_ada.reshape(depth, 1, n))


def _inproj_kernel(x_ref, sh_ref, sc_ref, w_ref, o_ref, h_ref):
    @pl.when(pl.program_id(2) == 0)
    def _():
        x = x_ref[0]
        mu = jnp.mean(x, axis=-1, keepdims=True)
        xc = x - mu
        var = jnp.mean(xc * xc, axis=-1, keepdims=True)
        h = (xc * lax.rsqrt(var + EPS)) * (1.0 + sc_ref[0]) + sh_ref[0]
        h_ref[...] = h.astype(BF16)

    o_ref[0] = _nt_dot(w_ref[...], h_ref[...]).astype(BF16)


def _input_projection(x, mod, w_t, tm, tn):
    b, s, _ = x.shape
    n = w_t.shape[0]
    return pl.pallas_call(
        _inproj_kernel,
        grid=(b, s // tm, n // tn),
        in_specs=[pl.BlockSpec((1, tm, D_MODEL), lambda i, t, j: (i, t, 0)),
                  pl.BlockSpec((1, 1, D_MODEL), lambda i, t, j: (i, 0, 0)),
                  pl.BlockSpec((1, 1, D_MODEL), lambda i, t, j: (i, 0, 1)),
                  pl.BlockSpec((tn, D_MODEL), lambda i, t, j: (j, 0))],
        out_specs=pl.BlockSpec((1, tn, tm), lambda i, t, j: (i, j, t)),
        out_shape=jax.ShapeDtypeStruct((b, n, s), BF16),
        scratch_shapes=[pltpu.VMEM((tm, D_MODEL), BF16)],
        compiler_params=_cparams(("parallel", "parallel", "arbitrary")),
        name="input_projection",
    )(x, mod, mod, w_t)


def _rms_rows(x, g):
    return x * lax.rsqrt(jnp.mean(x * x, axis=0, keepdims=True) + EPS) * g


def _mla_prep_kernel(a_ref, gq_ref, gkv_ref, wq_ref, wk_ref, wv_ref, rc_ref, rs_ref,
                     q_ref, k_ref, v_ref, *, q_scale):
    a = a_ref[0].astype(F32)
    tm = a.shape[1]
    qn = _rms_rows(a[0:A_Q_LORA], gq_ref[...]).astype(BF16)
    kvn = _rms_rows(a[A_Q_LORA:A_Q_LORA + A_KV_LORA], gkv_ref[...])
    kpe = a[A_Q_LORA + A_KV_LORA:A_COLS]
    rc = rc_ref[...]
    rs = rs_ref[...]
    half = A_ROPE // 2
    lo, mid, hi = A_NOPE, A_NOPE + half, A_NOPE + A_ROPE

    q = jnp.dot(wq_ref[...], qn, preferred_element_type=F32)
    q = q.reshape(N_HEADS, A_QK_PAD, tm)
    q_sw = jnp.concatenate([q[:, :lo], q[:, mid:hi], q[:, lo:mid], q[:, hi:]], axis=1)
    q = (q * rc[None] + q_sw * rs[None]) * q_scale
    q_ref[0] = q.reshape(N_HEADS * A_QK_PAD, tm).astype(BF16)

    kpe_sw = jnp.concatenate([kpe[half:], kpe[:half]], axis=0)
    kpe = kpe * rc[lo:hi] + kpe_sw * rs[lo:hi]
    xk = jnp.concatenate([kvn, kpe], axis=0).astype(BF16)
    k_ref[0] = jnp.dot(wk_ref[...], xk, preferred_element_type=F32).astype(BF16)
    v_ref[0] = jnp.dot(wv_ref[...], xk[:A_KV_LORA], preferred_element_type=F32).astype(BF16)


def _mla_prep(proj, gq, gkv, wq_t, wk_t, wv_t, rope_c, rope_s, tm):
    b, _, s = proj.shape
    hq = N_HEADS * A_QK_PAD
    full = lambda shape: pl.BlockSpec(shape, lambda i, t: (0,) * len(shape))
    return pl.pallas_call(
        functools.partial(_mla_prep_kernel, q_scale=(A_NOPE + A_ROPE) ** -0.5 * LOG2E),
        grid=(b, s // tm),
        in_specs=[pl.BlockSpec((1, A_ROWS, tm), lambda i, t: (i, A_ROW // A_ROWS, t)),
                  full((A_Q_LORA, 1)), full((A_KV_LORA, 1)),
                  full(wq_t.shape), full(wk_t.shape), full(wv_t.shape),
                  pl.BlockSpec((A_QK_PAD, tm), lambda i, t: (0, t)),
                  pl.BlockSpec((A_QK_PAD, tm), lambda i, t: (0, t))],
        out_specs=[pl.BlockSpec((1, hq, tm), lambda i, t: (i, 0, t)),
                   pl.BlockSpec((1, hq, tm), lambda i, t: (i, 0, t)),
                   pl.BlockSpec((1, GROUP_WIDTH, tm), lambda i, t: (i, 0, t))],
        out_shape=[jax.ShapeDtypeStruct((b, hq, s), BF16),
                   jax.ShapeDtypeStruct((b, hq, s), BF16),
                   jax.ShapeDtypeStruct((b, GROUP_WIDTH, s), BF16)],
        compiler_params=_cparams(("parallel", "parallel")),
        name="latent_attention_prep",
    )(proj, gq, gkv, wq_t, wk_t, wv_t, rope_c, rope_s)


def _axial_prep_kernel(x_ref, g_ref, rc_ref, rs_ref, q_ref, k_ref, *, q_scale):
    x = x_ref[0].astype(F32)
    tm = x.shape[1]
    nh = N_HEADS + D_KV_HEADS
    x = x.reshape(nh, HEAD_DIM, tm)
    x = x * lax.rsqrt(jnp.mean(x * x, axis=1, keepdims=True) + EPS) * g_ref[...]
    q4 = HEAD_DIM // 4
    x_sw = jnp.concatenate([x[:, q4:2 * q4], x[:, :q4], x[:, 3 * q4:], x[:, 2 * q4:3 * q4]], axis=1)
    x = x * rc_ref[...][None] + x_sw * rs_ref[...][None]
    q_ref[0] = (x[:N_HEADS] * q_scale).reshape(GROUP_WIDTH, tm).astype(BF16)
    k_ref[0] = x[N_HEADS:].reshape(D_KV_HEADS * HEAD_DIM, tm).astype(BF16)


def _axial_prep(proj, gains, rope_c, rope_s, tm):
    b, _, s = proj.shape
    rows = GROUP_WIDTH + D_KV_HEADS * HEAD_DIM
    return pl.pallas_call(
        functools.partial(_axial_prep_kernel, q_scale=HEAD_DIM ** -0.5 * LOG2E),
        grid=(b, s // tm),
        in_specs=[pl.BlockSpec((1, rows, tm), lambda i, t: (i, D_ROW // rows, t)),
                  pl.BlockSpec((N_HEADS + D_KV_HEADS, HEAD_DIM, 1), lambda i, t: (0, 0, 0)),
                  pl.BlockSpec((HEAD_DIM, tm), lambda i, t: (0, t)),
                  pl.BlockSpec((HEAD_DIM, tm), lambda i, t: (0, t))],
        out_specs=[pl.BlockSpec((1, GROUP_WIDTH, tm), lambda i, t: (i, 0, t)),
                   pl.BlockSpec((1, D_KV_HEADS * HEAD_DIM, tm), lambda i, t: (i, 0, t))],
        out_shape=[jax.ShapeDtypeStruct((b, GROUP_WIDTH, s), BF16),
                   jax.ShapeDtypeStruct((b, D_KV_HEADS * HEAD_DIM, s), BF16)],
        compiler_params=_cparams(("parallel", "parallel")),
        name="axial_attention_prep",
    )(proj, gains, rope_c, rope_s)


V_EXT = HEAD_DIM + 16


def _flash_kernel(q_ref, k_ref, v_ref, o_ref, s_ref, *, heads, groups, nsub, dq, tk, chunks):
    tq = q_ref.shape[2] // nsub
    nk = k_ref.shape[2] // tk
    chains = [(h, g, u) for h in range(heads) for g in range(groups) for u in range(nsub)]
    ones = jnp.ones((V_EXT - HEAD_DIM, tk), BF16)

    def scores(c, ci, slot):
        h, g, u = chains[ci]
        off = pl.multiple_of(c * tk, tk)
        qr = (h * groups + g) * dq
        s = _tn_dot(k_ref[0, h * dq:(h + 1) * dq, pl.ds(off, tk)], q_ref[0, qr:qr + dq, u * tq:(u + 1) * tq])
        s_ref[ci, slot] = s
        return jnp.max(s, axis=0, keepdims=True)

    def consume(c, ci, slot, cmax, m, acc):
        h = chains[ci][0]
        off = pl.multiple_of(c * tk, tk)
        v_ext = jnp.concatenate([v_ref[0, h * HEAD_DIM:(h + 1) * HEAD_DIM, pl.ds(off, tk)], ones], axis=0)
        m_new = jnp.maximum(m, cmax)
        alpha = jnp.exp2(m - m_new)
        p = jnp.exp2(s_ref[ci, slot] - m_new).astype(BF16)
        acc = alpha * acc + jnp.dot(v_ext, p, preferred_element_type=F32)
        return m_new, acc

    def chunk_group(c0, carry, last):
        state = list(carry)
        for t in range(chunks):
            for ci, (cmax, m, acc) in enumerate(state):
                cmax_next = cmax
                if not (last and t == chunks - 1):
                    cmax_next = scores(c0 + t + 1, ci, (t + 1) % 2)
                m, acc = consume(c0 + t, ci, t % 2, cmax, m, acc)
                state[ci] = (cmax_next, m, acc)
        return tuple(state)

    init = tuple((scores(0, ci, 0), jnp.full((1, tq), NEG, F32), jnp.zeros((V_EXT, tq), F32))
                 for ci in range(len(chains)))
    carry = lax.fori_loop(0, nk // chunks - 1, lambda i, carry: chunk_group(i * chunks, carry, False), init)
    carry = chunk_group(nk - chunks, carry, True)
    for (h, g, u), (_, _, acc) in zip(chains, carry):
        r0 = (h * groups + g) * HEAD_DIM
        o_ref[0, r0:r0 + HEAD_DIM, u * tq:(u + 1) * tq] = (acc[:HEAD_DIM] / acc[HEAD_DIM:HEAD_DIM + 1]).astype(BF16)


def _dense_attention(q, k, v, *, kv_heads, dq, k_row, v_row, heads, nsub, chunks, tq=512, tk=512):
    b, _, s = q.shape
    groups = N_HEADS // kv_heads
    assert chunks % 2 == 0 and (s // tk) % chunks == 0 and s % (tq * nsub) == 0 and kv_heads % heads == 0
    assert k_row % (heads * dq) == 0 and v_row % (heads * HEAD_DIM) == 0
    return pl.pallas_call(
        functools.partial(_flash_kernel, heads=heads, groups=groups, nsub=nsub, dq=dq, tk=tk, chunks=chunks),
        grid=(b, kv_heads // heads, s // (tq * nsub)),
        in_specs=[pl.BlockSpec((1, heads * groups * dq, tq * nsub), lambda i, h, t: (i, h, t)),
                  pl.BlockSpec((1, heads * dq, s), lambda i, h, t: (i, k_row // (heads * dq) + h, 0)),
                  pl.BlockSpec((1, heads * HEAD_DIM, s), lambda i, h, t: (i, v_row // (heads * HEAD_DIM) + h, 0))],
        out_specs=pl.BlockSpec((1, heads * groups * HEAD_DIM, tq * nsub), lambda i, h, t: (i, h, t)),
        out_shape=jax.ShapeDtypeStruct((b, GROUP_WIDTH, s), BF16),
        scratch_shapes=[pltpu.VMEM((heads * groups * nsub, 2, tk, tq), F32)],
        compiler_params=_cparams(("parallel", "parallel", "arbitrary")),
        name="dense_attention",
    )(q, k, v)


def _t5_bucket(rel):
    half = NUM_BUCKETS // 2
    exact = half // 2
    n = jnp.abs(rel)
    large = exact + (jnp.log(jnp.maximum(n, 1).astype(F32) / exact)
                     / math.log(T5_MAX_DISTANCE / exact) * (half - exact)).astype(jnp.int32)
    large = jnp.minimum(large, half - 1)
    return jnp.where(rel > 0, half, 0) + jnp.where(n < exact, n, large)


def _band_buckets(w, dilation, kw, qb):
    kk = jnp.arange(kw)[:, None]
    qq = jnp.arange(qb)[None, :]
    out = []
    for shift in (0, BAND_HALO, kw - qb):
        rel = kk - shift - qq
        out.append(jnp.where(jnp.abs(rel) <= w, _t5_bucket(dilation * rel), -1))
    return jnp.stack(out).astype(jnp.int32)


def _bias_kernel(bucket_ref, tab_ref, o_ref, *, groups):
    h = pl.program_id(0)
    bk = bucket_ref[...]
    qb = bk.shape[2]
    for g in range(groups):
        acc = jnp.zeros(bk.shape, F32)
        for b in range(NUM_BUCKETS):
            acc = jnp.where(bk == b, tab_ref[b, h * groups + g], acc)
        o_ref[:, 0, :, g * qb:(g + 1) * qb] = jnp.where(bk < 0, NEG, acc * LOG2E)


def _band_bias(table, w, dilation, seq, qb, kv_heads):
    groups = N_HEADS // kv_heads
    kw = min(seq, qb + 2 * BAND_HALO)
    buckets = _band_buckets(w, dilation, kw, qb)
    return pl.pallas_call(
        functools.partial(_bias_kernel, groups=groups),
        grid=(kv_heads,),
        in_specs=[pl.BlockSpec((3, kw, qb), lambda h: (0, 0, 0)),
                  pl.BlockSpec(memory_space=pltpu.SMEM)],
        out_specs=pl.BlockSpec((3, 1, kw, groups * qb), lambda h: (0, h, 0, 0)),
        out_shape=jax.ShapeDtypeStruct((3, kv_heads, kw, groups * qb), F32),
        compiler_params=_cparams(("parallel",)),
        name="relative_bias_tables",
    )(buckets, table)


def _banded_kernel(q_ref, k_ref, v_ref, tab_ref, *rest, groups, heads, kw, has_sink):
    if has_sink:
        sink_ref, o_ref, lse_ref = rest
    else:
        o_ref, lse_ref = rest
    nb, _, tq = q_ref.shape
    seq = k_ref.shape[2]
    t0 = pl.program_id(2) * tq
    qb = tab_ref.shape[3] // groups

    def window(j):
        qoff = pl.multiple_of(j * qb, qb)
        q0 = t0 + qoff
        sel = jnp.where(q0 == 0, 0, jnp.where(q0 == seq - qb, 2, 1))
        ks = pl.multiple_of(jnp.clip(q0 - BAND_HALO, 0, seq - kw), BAND_HALO)
        return qoff, sel, ks

    def scores(bb, j, hb):
        qoff, sel, ks = window(j)
        k_t = k_ref[bb, hb * HEAD_DIM:(hb + 1) * HEAD_DIM, pl.ds(ks, kw)]
        r0 = hb * groups * HEAD_DIM
        q_cat = jnp.concatenate(
            [q_ref[bb, r0 + g * HEAD_DIM:r0 + (g + 1) * HEAD_DIM, pl.ds(qoff, qb)]
             for g in range(groups)], axis=1)
        return _tn_dot(k_t, q_cat) + tab_ref[sel, hb]

    ones = jnp.ones((V_EXT - HEAD_DIM, kw), BF16)

    def finish(bb, j, hb, s):
        qoff, _, ks = window(j)
        v_ext = jnp.concatenate([v_ref[bb, hb * HEAD_DIM:(hb + 1) * HEAD_DIM, pl.ds(ks, kw)], ones], axis=0)
        m = jnp.max(s, axis=0, keepdims=True)
        if has_sink:
            sink = sink_ref[hb]
            m = jnp.maximum(m, sink)
        p = jnp.exp2(s - m).astype(BF16)
        ol = jnp.dot(v_ext, p, preferred_element_type=F32)
        l = ol[HEAD_DIM:HEAD_DIM + 1]
        if has_sink:
            l = l + jnp.exp2(sink - m)
        return qoff, (ol[:HEAD_DIM] / l).astype(o_ref.dtype), m + jnp.log2(l)

    nsub = tq // qb
    per_body = max(1, min(nsub, BAND_UNITS // heads))
    assert nsub % per_body == 0
    ngroups = nsub // per_body

    def body(idx, carry):
        bb = idx // ngroups
        jj = idx % ngroups
        units = [(bb, jj * per_body + u, hb) for u in range(per_body) for hb in range(heads)]
        s_vals = [scores(*unit) for unit in units[:BAND_AHEAD]]
        done = []
        for i, unit in enumerate(units):
            if i + BAND_AHEAD < len(units):
                s_vals.append(scores(*units[i + BAND_AHEAD]))
            done.append(finish(*unit, s_vals[i]))
        for (_, _, hb), (qoff, o, lse) in zip(units, done):
            r0 = hb * groups * HEAD_DIM
            for g in range(groups):
                o_ref[bb, r0 + g * HEAD_DIM:r0 + (g + 1) * HEAD_DIM, pl.ds(qoff, qb)] = o[:, g * qb:(g + 1) * qb]
                lse_ref[bb, hb, g:g + 1, pl.ds(qoff, qb)] = lse[:, g * qb:(g + 1) * qb]
        return carry

    lax.fori_loop(0, nb * ngroups, body, 0)


def _largest_divisor(n, cap, ok=lambda d: True):
    return max(d for d in range(1, n + 1) if n % d == 0 and d <= max(cap, 1) and ok(d))


def _banded_attention(q, k, v, bias, sink, *, kv_heads, q_row, k_row, v_row):
    b, _, seq = q.shape
    groups = N_HEADS // kv_heads
    kw = bias.shape[2]
    tq = min(seq, BAND_TQ)
    def aligned(d):
        return q_row % (d * groups * HEAD_DIM) == 0 and k_row % (d * HEAD_DIM) == 0 and v_row % (d * HEAD_DIM) == 0

    per_head = HEAD_DIM * (2 * groups * tq + 2 * seq)
    hb = _largest_divisor(kv_heads, BAND_STEP_ELEMS // per_head, aligned)
    nb = _largest_divisor(b, BAND_STEP_ELEMS // (per_head * hb))
    qrows, krows = hb * groups * HEAD_DIM, hb * HEAD_DIM
    in_specs = [pl.BlockSpec((nb, qrows, tq), lambda i, h, t: (i, q_row // qrows + h, t)),
                pl.BlockSpec((nb, krows, seq), lambda i, h, t: (i, k_row // krows + h, 0)),
                pl.BlockSpec((nb, krows, seq), lambda i, h, t: (i, v_row // krows + h, 0)),
                pl.BlockSpec((3, hb, kw, bias.shape[3]), lambda i, h, t: (0, h, 0, 0))]
    args = [q, k, v, bias]
    if sink is not None:
        in_specs.append(pl.BlockSpec((hb, 1, bias.shape[3]), lambda i, h, t: (h, 0, 0)))
        args.append(sink)
    return pl.pallas_call(
        functools.partial(_banded_kernel, groups=groups, heads=hb, kw=kw, has_sink=sink is not None),
        grid=(b // nb, kv_heads // hb, seq // tq),
        in_specs=in_specs,
        out_specs=[pl.BlockSpec((nb, qrows, tq), lambda i, h, t: (i, h, t)),
                   pl.BlockSpec((nb, hb, groups, tq), lambda i, h, t: (i, h, 0, t))],
        out_shape=[jax.ShapeDtypeStruct((b, GROUP_WIDTH, seq), BF16),
                   jax.ShapeDtypeStruct((b, kv_heads, groups, seq), F32)],
        compiler_params=_cparams(("parallel", "parallel", "arbitrary")),
        name="banded_attention",
    )(*args)


def _mix_kernel(o1_ref, o2_ref, o3_ref, l1_ref, l2_ref, l3_ref, y_ref):
    for h in range(N_HEADS):
        a1, a2, a3 = l1_ref[0, h], l2_ref[0, h], l3_ref[0, h]
        mx = jnp.maximum(jnp.maximum(a1, a2), a3)
        e1, e2, e3 = jnp.exp2(a1 - mx), jnp.exp2(a2 - mx), jnp.exp2(a3 - mx)
        inv = 1.0 / (e1 + e2 + e3)
        rows = slice(h * HEAD_DIM, (h + 1) * HEAD_DIM)
        y = ((e1 * inv) * o1_ref[0, rows, :].astype(F32) + (e2 * inv) * o2_ref[0, rows, :].astype(F32)
             + (e3 * inv) * o3_ref[0, rows, :].astype(F32))
        y_ref[0, rows, :] = y.astype(BF16)


def _dilated_mix(outs, lses, tm):
    b, _, s = outs[0].shape
    ospec = pl.BlockSpec((1, GROUP_WIDTH, tm), lambda i, t: (i, 0, t))
    lspec = pl.BlockSpec((1, N_HEADS, 1, tm), lambda i, t: (i, 0, 0, t))
    return pl.pallas_call(
        _mix_kernel,
        grid=(b, s // tm),
        in_specs=[ospec] * 3 + [lspec] * 3,
        out_specs=ospec,
        out_shape=jax.ShapeDtypeStruct((b, GROUP_WIDTH, s), BF16),
        compiler_params=_cparams(("parallel", "parallel")),
        name="dilated_mixture",
    )(*outs, *lses)


def _outproj_kernel(ya_ref, yb_ref, yc_ref, yd_ref, z_ref, x_ref, gate_ref, w_ref, g_ref, b_ref, o_ref):
    acc = None
    for i, y_ref in enumerate((ya_ref, yb_ref, yc_ref, yd_ref)):
        rows = slice(i * GROUP_WIDTH, (i + 1) * GROUP_WIDTH)
        z = z_ref[0, rows, :].astype(F32)
        gated = (y_ref[0].astype(F32) * _silu(z)).astype(BF16)
        part = _tn_dot(gated, w_ref[rows, :])
        acc = part if acc is None else acc + part
    r = ALPHA * x_ref[0] + gate_ref[0] * acc
    mu = jnp.mean(r, axis=-1, keepdims=True)
    rc = r - mu
    var = jnp.mean(rc * rc, axis=-1, keepdims=True)
    o_ref[0] = (rc * lax.rsqrt(var + EPS)) * g_ref[...] + b_ref[...]


def _output_projection(ys, proj, x, mod, w_out, ln_g, ln_b, tm):
    b, s, _ = x.shape
    yspec = pl.BlockSpec((1, GROUP_WIDTH, tm), lambda i, t: (i, 0, t))
    return pl.pallas_call(
        _outproj_kernel,
        grid=(b, s // tm),
        in_specs=[yspec] * 4 + [
            pl.BlockSpec((1, D_MIX, tm), lambda i, t: (i, Z_ROW // D_MIX, t)),
            pl.BlockSpec((1, tm, D_MODEL), lambda i, t: (i, t, 0)),
            pl.BlockSpec((1, 1, D_MODEL), lambda i, t: (i, 0, 2)),
            pl.BlockSpec((D_MIX, D_MODEL), lambda i, t: (0, 0)),
            pl.BlockSpec((1, D_MODEL), lambda i, t: (0, 0)),
            pl.BlockSpec((1, D_MODEL), lambda i, t: (0, 0))],
        out_specs=pl.BlockSpec((1, tm, D_MODEL), lambda i, t: (i, t, 0)),
        out_shape=jax.ShapeDtypeStruct((b, s, D_MODEL), F32),
        compiler_params=_cparams(("parallel", "parallel")),
        name="output_projection",
    )(*ys, proj, x, mod, w_out, ln_g, ln_b)


def _rope_tables(seq):
    pos = jnp.arange(seq)
    inv = ROPE_THETA ** (-jnp.arange(0, A_ROPE, 2, dtype=F32) / A_ROPE)

    def cs(p):
        ang = p.astype(F32)[:, None] * inv[None, :]
        return jnp.cos(ang).T, jnp.sin(ang).T

    c, sn = cs(pos)
    ones, zeros = jnp.ones((A_NOPE, seq), F32), jnp.zeros((A_NOPE, seq), F32)
    pad = jnp.zeros((A_QK_PAD - A_NOPE - A_ROPE, seq), F32)
    a_c = jnp.concatenate([ones, c, c, pad], axis=0)
    a_s = jnp.concatenate([zeros, -sn, sn, pad], axis=0)
    cr, sr = cs(pos // GRID_W)
    cc, sc = cs(pos % GRID_W)
    d_c = jnp.concatenate([cr, cr, cc, cc], axis=0)
    d_s = jnp.concatenate([-sr, sr, -sc, sc], axis=0)
    return a_c, a_s, d_c, d_s


def _layer_weights(l, w_in, a_q_norm, a_q_up, a_kv_norm, a_kv_up, b_sink, d_q_norm, d_k_norm, w_out,
                   ln_g, ln_b):
    wa, wb, wc, wd, wz = jnp.split(w_in[l], np.cumsum([A_COLS, B_COLS, C_COLS, D_COLS]).tolist(), axis=1)
    wa = jnp.pad(wa, ((0, 0), (0, A_ROWS - A_COLS)))
    band_scale = jnp.concatenate([jnp.full((GROUP_WIDTH,), HEAD_DIM ** -0.5 * LOG2E, F32),
                                  jnp.ones((B_COLS - GROUP_WIDTH,), F32)])
    wb = wb * band_scale[None, :]
    wc = wc * jnp.pad(band_scale[:GROUP_WIDTH], (0, C_COLS - GROUP_WIDTH), constant_values=1.0)[None, :]
    w_t = jnp.concatenate([wz, wa, wb, wd, wc], axis=1).T.astype(BF16)
    dqk = A_NOPE + A_ROPE
    wq = jnp.pad(a_q_up[l].reshape(A_Q_LORA, N_HEADS, dqk), ((0, 0), (0, 0), (0, A_QK_PAD - dqk)))
    wq_t = wq.reshape(A_Q_LORA, N_HEADS * A_QK_PAD).T.astype(BF16)
    kvu = a_kv_up[l].reshape(A_KV_LORA, N_HEADS, A_NOPE + HEAD_DIM)
    wk_top = jnp.pad(kvu[:, :, :A_NOPE], ((0, 0), (0, 0), (0, A_QK_PAD - A_NOPE)))
    sel = jnp.zeros((A_ROPE, N_HEADS, A_QK_PAD), F32)
    sel = sel.at[:, :, A_NOPE:A_NOPE + A_ROPE].set(jnp.eye(A_ROPE, dtype=F32)[:, None, :])
    wk_t = jnp.concatenate([wk_top, sel], axis=0).reshape(A_KV_LORA + A_ROPE, -1).T.astype(BF16)
    wv_t = kvu[:, :, A_NOPE:].reshape(A_KV_LORA, GROUP_WIDTH).T.astype(BF16)
    d_gain = jnp.concatenate([jnp.broadcast_to(d_q_norm[l], (N_HEADS, HEAD_DIM)),
                              jnp.broadcast_to(d_k_norm[l], (D_KV_HEADS, HEAD_DIM))])[:, :, None]
    groups = N_HEADS // B_KV_HEADS
    sink = jnp.repeat(b_sink[l].reshape(B_KV_HEADS, 1, groups) * LOG2E, BAND_QB, axis=2)
    return dict(w_t=w_t, gq=a_q_norm[l][:, None], gkv=a_kv_norm[l][:, None], wq_t=wq_t, wk_t=wk_t,
                wv_t=wv_t, d_gain=d_gain, sink=sink, w_out=w_out[l].astype(BF16),
                ln_g=ln_g[l][None, :], ln_b=ln_b[l][None, :])


def _to_strided(t, r):
    b, f, s = t.shape
    return t.reshape(b, f, s // r, r).transpose(0, 3, 1, 2).reshape(b * r, f, s // r)


def _from_strided(t, b, r):
    rest = t.shape[1:-1]
    sr = t.shape[-1]
    t = jnp.moveaxis(t.reshape((b, r) + rest + (sr,)), 1, -1)
    return t.reshape((b,) + rest + (sr * r,))


def _pick(n, pref):
    return min(n, pref)


def _layer(x, mod, wts, bias_b, bias_c, ropes):
    b, s, _ = x.shape
    a_c, a_s, d_c, d_s = ropes
    proj = _input_projection(x, mod, wts["w_t"], tm=_pick(s, 1024), tn=1088)

    tp = _pick(s, 512)
    q_a, k_a, v_a = _mla_prep(proj, wts["gq"], wts["gkv"], wts["wq_t"], wts["wk_t"], wts["wv_t"],
                              a_c, a_s, tp)
    short = s <= 2048
    y_a = _dense_attention(q_a, k_a, v_a, kv_heads=N_HEADS, dq=A_QK_PAD, k_row=0, v_row=0,
                           heads=1, nsub=4, chunks=8, tk=256 if short else 512)

    y_b, _ = _banded_attention(proj, proj, proj, bias_b, wts["sink"], kv_heads=B_KV_HEADS,
                               q_row=B_ROW, k_row=B_ROW + GROUP_WIDTH,
                               v_row=B_ROW + GROUP_WIDTH + B_KV_HEADS * HEAD_DIM)

    outs, lses = [], []
    for (window, r), bias in zip(DILATED_PAIRS, bias_c):
        if r == 1:
            o, lse = _banded_attention(proj, proj, proj, bias, None, kv_heads=N_HEADS,
                                       q_row=C_ROW, k_row=C_ROW + GROUP_WIDTH, v_row=C_ROW + 2 * GROUP_WIDTH)
        else:
            qkv = _to_strided(proj[:, C_ROW:, :], r)
            o, lse = _banded_attention(qkv, qkv, qkv, bias, None, kv_heads=N_HEADS,
                                       q_row=0, k_row=GROUP_WIDTH, v_row=2 * GROUP_WIDTH)
            o, lse = _from_strided(o, b, r), _from_strided(lse, b, r)
        outs.append(o)
        lses.append(lse)
    y_c = _dilated_mix(outs, lses, _pick(s, 1024))

    q_d, k_d = _axial_prep(proj, wts["d_gain"], d_c, d_s, tp)
    y_d = _dense_attention(q_d, k_d, proj, kv_heads=D_KV_HEADS, dq=HEAD_DIM, k_row=0,
                           v_row=D_ROW + GROUP_WIDTH + D_KV_HEADS * HEAD_DIM,
                           heads=1, nsub=2, chunks=8, tk=256 if short else 512)

    return _output_projection((y_a, y_b, y_c, y_d), proj, x, mod, wts["w_out"], wts["ln_g"], wts["ln_b"],
                              tm=_pick(s, 256))


def _trunk(x, mods, layer_wts, rel_bias):
    b, s, _ = x.shape
    ropes = _rope_tables(s)
    bias_b = _band_bias(rel_bias[:, :N_HEADS], B_WINDOW, 1, s, BAND_QB, B_KV_HEADS)
    bias_c = [_band_bias(rel_bias[:, N_HEADS:], window // (2 * r), r, s // r, BAND_QB, N_HEADS)
              for window, r in DILATED_PAIRS]
    for l in range(DEPTH):
        x = _layer(x, mods[l], layer_wts[l], bias_b, bias_c, ropes)
    return x


def kernel(x_prompt, x_sample, c_prompt, c_sample, w_ada, b_ada, w_in, a_q_norm, a_q_up, a_kv_norm, a_kv_up, b_sink, d_q_norm, d_k_norm, w_out, ln_g, ln_b, rel_bias):
    bp, bs = c_prompt.shape[0], c_sample.shape[0]
    rows = -(-(bp + bs) // 16) * 16
    c_all = jnp.pad(jnp.concatenate([c_prompt, c_sample], axis=0), ((0, rows - bp - bs), (0, 0)))
    mod = _modulation(c_all, w_ada, b_ada)
    layer_wts = [_layer_weights(l, w_in, a_q_norm, a_q_up, a_kv_norm, a_kv_up, b_sink, d_q_norm,
                                d_k_norm, w_out, ln_g, ln_b) for l in range(DEPTH)]
    mods_p = [mod[l, :bp].reshape(bp, 1, -1) for l in range(DEPTH)]
    mods_s = [mod[l, bp:bp + bs].reshape(bs, 1, -1) for l in range(DEPTH)]
    y_prompt = _trunk(x_prompt, mods_p, layer_wts, rel_bias)
    y_sample = _trunk(x_sample, mods_s, layer_wts, rel_bias)
    return (y_prompt, y_sample)
```

```python
import functools
import math

import numpy as np
import jax
import jax.numpy as jnp
from jax import lax
from jax.experimental import pallas as pl
from jax.experimental.pallas import tpu as pltpu

F32 = jnp.float32
BF16 = jnp.bfloat16

D_MODEL = 2048
DEPTH = 2
HEAD_DIM = 64
N_HEADS = 12
A_Q_LORA = 384
A_KV_LORA = 256
A_NOPE = 64
A_ROPE = 32
A_QK_PAD = 128
B_KV_HEADS = 4
B_WINDOW = 128
DILATED_PAIRS = ((128, 1), (512, 4), (2048, 16))
D_KV_HEADS = 4
GROUP_WIDTH = N_HEADS * HEAD_DIM
D_MIX = 4 * GROUP_WIDTH
A_COLS = A_Q_LORA + A_KV_LORA + A_ROPE
B_COLS = (N_HEADS + 2 * B_KV_HEADS) * HEAD_DIM
C_COLS = 3 * N_HEADS * HEAD_DIM
D_COLS = (N_HEADS + 2 * D_KV_HEADS) * HEAD_DIM
GRID_W = 64
NUM_BUCKETS = 32
T5_MAX_DISTANCE = 1024
ROPE_THETA = 10000.0
EPS = 1e-6
NEG = -1e30
ALPHA = (2 * DEPTH) ** 0.25
LOG2E = 1.4426950408889634

Z_ROW = 0
A_ROW = D_MIX
A_ROWS = 768
B_ROW = A_ROW + A_ROWS
D_ROW = B_ROW + B_COLS
C_ROW = D_ROW + D_COLS
N_PROJ = C_ROW + C_COLS

LANE = 128
BAND_QB = 128
BAND_HALO = 128
BAND_UNITS = 16
BAND_AHEAD = 3
BAND_TQ = 2048
BAND_STEP_ELEMS = 6 * 2 ** 20
VMEM_LIMIT = 52 * 2 ** 20


def _cparams(sem):
    return pltpu.CompilerParams(dimension_semantics=sem, vmem_limit_bytes=VMEM_LIMIT)


def _silu(x):
    return x / (1.0 + jnp.exp(-x))


def _tn_dot(a, b):
    return lax.dot_general(a, b, (((0,), (0,)), ((), ())), preferred_element_type=F32)


def _nt_dot(a, b):
    return lax.dot_general(a, b, (((1,), (1,)), ((), ())), preferred_element_type=F32)


def _mod_kernel(c_ref, w_ref, b_ref, o_ref):
    sc = _silu(c_ref[...]).astype(BF16)
    o_ref[0] = jnp.dot(sc, w_ref[0].astype(BF16), preferred_element_type=F32) + b_ref[0]


def _modulation(c_all, w_ada, b_ada):
    depth, _, n = w_ada.shape
    rows = c_all.shape[0]
    tn = 768
    return pl.pallas_call(
        _mod_kernel,
        grid=(depth, n // tn),
        in_specs=[pl.BlockSpec((rows, D_MODEL), lambda l, j: (0, 0)),
                  pl.BlockSpec((1, D_MODEL, tn), lambda l, j: (l, 0, j)),
                  pl.BlockSpec((1, 1, tn), lambda l, j: (l, 0, j))],
        out_specs=pl.BlockSpec((1, rows, tn), lambda l, j: (l, 0, j)),
        out_shape=jax.ShapeDtypeStruct((depth, rows, n), F32),
        compiler_params=_cparams(("parallel", "parallel")),
        name="adaln_modulation",
    )(c_all, w_ada, b_ada.reshape(depth, 1, n))


def _inproj_kernel(x_ref, sh_ref, sc_ref, w_ref, o_ref, h_ref):
    @pl.when(pl.program_id(2) == 0)
    def _():
        x = x_ref[0]
        mu = jnp.mean(x, axis=-1, keepdims=True)
        xc = x - mu
        var = jnp.mean(xc * xc, axis=-1, keepdims=True)
        h = (xc * lax.rsqrt(var + EPS)) * (1.0 + sc_ref[0]) + sh_ref[0]
        h_ref[...] = h.astype(BF16)

    o_ref[0] = _nt_dot(w_ref[...], h_ref[...]).astype(BF16)


def _input_projection(x, mod, w_t, tm, tn):
    b, s, _ = x.shape
    n = w_t.shape[0]
    return pl.pallas_call(
        _inproj_kernel,
        grid=(b, s // tm, n // tn),
        in_specs=[pl.BlockSpec((1, tm, D_MODEL), lambda i, t, j: (i, t, 0)),
                  pl.BlockSpec((1, 1, D_MODEL), lambda i, t, j: (i, 0, 0)),
                  pl.BlockSpec((1, 1, D_MODEL), lambda i, t, j: (i, 0, 1)),
                  pl.BlockSpec((tn, D_MODEL), lambda i, t, j: (j, 0))],
        out_specs=pl.BlockSpec((1, tn, tm), lambda i, t, j: (i, j, t)),
        out_shape=jax.ShapeDtypeStruct((b, n, s), BF16),
        scratch_shapes=[pltpu.VMEM((tm, D_MODEL), BF16)],
        compiler_params=_cparams(("parallel", "parallel", "arbitrary")),
        name="input_projection",
    )(x, mod, mod, w_t)


def _rms_rows(x, g):
    return x * lax.rsqrt(jnp.mean(x * x, axis=0, keepdims=True) + EPS) * g


def _mla_prep_kernel(a_ref, gq_ref, gkv_ref, wq_ref, wk_ref, wv_ref, rc_ref, rs_ref,
                     q_ref, k_ref, v_ref, *, q_scale):
    a = a_ref[0].astype(F32)
    tm = a.shape[1]
    qn = _rms_rows(a[0:A_Q_LORA], gq_ref[...]).astype(BF16)
    kvn = _rms_rows(a[A_Q_LORA:A_Q_LORA + A_KV_LORA], gkv_ref[...])
    kpe = a[A_Q_LORA + A_KV_LORA:A_COLS]
    rc = rc_ref[...]
    rs = rs_ref[...]
    half = A_ROPE // 2
    lo, mid, hi = A_NOPE, A_NOPE + half, A_NOPE + A_ROPE

    q = jnp.dot(wq_ref[...], qn, preferred_element_type=F32)
    q = q.reshape(N_HEADS, A_QK_PAD, tm)
    q_sw = jnp.concatenate([q[:, :lo], q[:, mid:hi], q[:, lo:mid], q[:, hi:]], axis=1)
    q = (q * rc[None] + q_sw * rs[None]) * q_scale
    q_ref[0] = q.reshape(N_HEADS * A_QK_PAD, tm).astype(BF16)

    kpe_sw = jnp.concatenate([kpe[half:], kpe[:half]], axis=0)
    kpe = kpe * rc[lo:hi] + kpe_sw * rs[lo:hi]
    xk = jnp.concatenate([kvn, kpe], axis=0).astype(BF16)
    k_ref[0] = jnp.dot(wk_ref[...], xk, preferred_element_type=F32).astype(BF16)
    v_ref[0] = jnp.dot(wv_ref[...], xk[:A_KV_LORA], preferred_element_type=F32).astype(BF16)


def _mla_prep(proj, gq, gkv, wq_t, wk_t, wv_t, rope_c, rope_s, tm):
    b, _, s = proj.shape
    hq = N_HEADS * A_QK_PAD
    full = lambda shape: pl.BlockSpec(shape, lambda i, t: (0,) * len(shape))
    return pl.pallas_call(
        functools.partial(_mla_prep_kernel, q_scale=(A_NOPE + A_ROPE) ** -0.5 * LOG2E),
        grid=(b, s // tm),
        in_specs=[pl.BlockSpec((1, A_ROWS, tm), lambda i, t: (i, A_ROW // A_ROWS, t)),
                  full((A_Q_LORA, 1)), full((A_KV_LORA, 1)),
                  full(wq_t.shape), full(wk_t.shape), full(wv_t.shape),
                  pl.BlockSpec((A_QK_PAD, tm), lambda i, t: (0, t)),
                  pl.BlockSpec((A_QK_PAD, tm), lambda i, t: (0, t))],
        out_specs=[pl.BlockSpec((1, hq, tm), lambda i, t: (i, 0, t)),
                   pl.BlockSpec((1, hq, tm), lambda i, t: (i, 0, t)),
                   pl.BlockSpec((1, GROUP_WIDTH, tm), lambda i, t: (i, 0, t))],
        out_shape=[jax.ShapeDtypeStruct((b, hq, s), BF16),
                   jax.ShapeDtypeStruct((b, hq, s), BF16),
                   jax.ShapeDtypeStruct((b, GROUP_WIDTH, s), BF16)],
        compiler_params=_cparams(("parallel", "parallel")),
        name="latent_attention_prep",
    )(proj, gq, gkv, wq_t, wk_t, wv_t, rope_c, rope_s)


def _axial_prep_kernel(x_ref, g_ref, rc_ref, rs_ref, q_ref, k_ref, *, q_scale):
    x = x_ref[0].astype(F32)
    tm = x.shape[1]
    nh = N_HEADS + D_KV_HEADS
    x = x.reshape(nh, HEAD_DIM, tm)
    x = x * lax.rsqrt(jnp.mean(x * x, axis=1, keepdims=True) + EPS) * g_ref[...]
    q4 = HEAD_DIM // 4
    x_sw = jnp.concatenate([x[:, q4:2 * q4], x[:, :q4], x[:, 3 * q4:], x[:, 2 * q4:3 * q4]], axis=1)
    x = x * rc_ref[...][None] + x_sw * rs_ref[...][None]
    q_ref[0] = (x[:N_HEADS] * q_scale).reshape(GROUP_WIDTH, tm).astype(BF16)
    k_ref[0] = x[N_HEADS:].reshape(D_KV_HEADS * HEAD_DIM, tm).astype(BF16)


def _axial_prep(proj, gains, rope_c, rope_s, tm):
    b, _, s = proj.shape
    rows = GROUP_WIDTH + D_KV_HEADS * HEAD_DIM
    return pl.pallas_call(
        functools.partial(_axial_prep_kernel, q_scale=HEAD_DIM ** -0.5 * LOG2E),
        grid=(b, s // tm),
        in_specs=[pl.BlockSpec((1, rows, tm), lambda i, t: (i, D_ROW // rows, t)),
                  pl.BlockSpec((N_HEADS + D_KV_HEADS, HEAD_DIM, 1), lambda i, t: (0, 0, 0)),
                  pl.BlockSpec((HEAD_DIM, tm), lambda i, t: (0, t)),
                  pl.BlockSpec((HEAD_DIM, tm), lambda i, t: (0, t))],
        out_specs=[pl.BlockSpec((1, GROUP_WIDTH, tm), lambda i, t: (i, 0, t)),
                   pl.BlockSpec((1, D_KV_HEADS * HEAD_DIM, tm), lambda i, t: (i, 0, t))],
        out_shape=[jax.ShapeDtypeStruct((b, GROUP_WIDTH, s), BF16),
                   jax.ShapeDtypeStruct((b, D_KV_HEADS * HEAD_DIM, s), BF16)],
        compiler_params=_cparams(("parallel", "parallel")),
        name="axial_attention_prep",
    )(proj, gains, rope_c, rope_s)


V_EXT = HEAD_DIM + 16


def _flash_kernel(q_ref, k_ref, v_ref, o_ref, s_ref, *, heads, groups, nsub, dq, tk, chunks):
    tq = q_ref.shape[2] // nsub
    nk = k_ref.shape[2] // tk
    chains = [(h, g, u) for h in range(heads) for g in range(groups) for u in range(nsub)]
    ones = jnp.ones((V_EXT - HEAD_DIM, tk), BF16)

    def scores(c, ci, slot):
        h, g, u = chains[ci]
        off = pl.multiple_of(c * tk, tk)
        qr = (h * groups + g) * dq
        s = _tn_dot(k_ref[0, h * dq:(h + 1) * dq, pl.ds(off, tk)], q_ref[0, qr:qr + dq, u * tq:(u + 1) * tq])
        s_ref[ci, slot] = s
        return jnp.max(s, axis=0, keepdims=True)

    def consume(c, ci, slot, cmax, m, acc):
        h = chains[ci][0]
        off = pl.multiple_of(c * tk, tk)
        v_ext = jnp.concatenate([v_ref[0, h * HEAD_DIM:(h + 1) * HEAD_DIM, pl.ds(off, tk)], ones], axis=0)
        m_new = jnp.maximum(m, cmax)
        alpha = jnp.exp2(m - m_new)
        p = jnp.exp2(s_ref[ci, slot] - m_new).astype(BF16)
        acc = alpha * acc + jnp.dot(v_ext, p, preferred_element_type=F32)
        return m_new, acc

    def chunk_group(c0, carry, last):
        state = list(carry)
        for t in range(chunks):
            for ci, (cmax, m, acc) in enumerate(state):
                cmax_next = cmax
                if not (last and t == chunks - 1):
                    cmax_next = scores(c0 + t + 1, ci, (t + 1) % 2)
                m, acc = consume(c0 + t, ci, t % 2, cmax, m, acc)
                state[ci] = (cmax_next, m, acc)
        return tuple(state)

    init = tuple((scores(0, ci, 0), jnp.full((1, tq), NEG, F32), jnp.zeros((V_EXT, tq), F32))
                 for ci in range(len(chains)))
    carry = lax.fori_loop(0, nk // chunks - 1, lambda i, carry: chunk_group(i * chunks, carry, False), init)
    carry = chunk_group(nk - chunks, carry, True)
    for (h, g, u), (_, _, acc) in zip(chains, carry):
        r0 = (h * groups + g) * HEAD_DIM
        o_ref[0, r0:r0 + HEAD_DIM, u * tq:(u + 1) * tq] = (acc[:HEAD_DIM] / acc[HEAD_DIM:HEAD_DIM + 1]).astype(BF16)


def _dense_attention(q, k, v, *, kv_heads, dq, k_row, v_row, heads, nsub, chunks, tq=512, tk=512):
    b, _, s = q.shape
    groups = N_HEADS // kv_heads
    assert chunks % 2 == 0 and (s // tk) % chunks == 0 and s % (tq * nsub) == 0 and kv_heads % heads == 0
    assert k_row % (heads * dq) == 0 and v_row % (heads * HEAD_DIM) == 0
    return pl.pallas_call(
        functools.partial(_flash_kernel, heads=heads, groups=groups, nsub=nsub, dq=dq, tk=tk, chunks=chunks),
        grid=(b, kv_heads // heads, s // (tq * nsub)),
        in_specs=[pl.BlockSpec((1, heads * groups * dq, tq * nsub), lambda i, h, t: (i, h, t)),
                  pl.BlockSpec((1, heads * dq, s), lambda i, h, t: (i, k_row // (heads * dq) + h, 0)),
                  pl.BlockSpec((1, heads * HEAD_DIM, s), lambda i, h, t: (i, v_row // (heads * HEAD_DIM) + h, 0))],
        out_specs=pl.BlockSpec((1, heads * groups * HEAD_DIM, tq * nsub), lambda i, h, t: (i, h, t)),
        out_shape=jax.ShapeDtypeStruct((b, GROUP_WIDTH, s), BF16),
        scratch_shapes=[pltpu.VMEM((heads * groups * nsub, 2, tk, tq), F32)],
        compiler_params=_cparams(("parallel", "parallel", "arbitrary")),
        name="dense_attention",
    )(q, k, v)


def _t5_bucket(rel):
    half = NUM_BUCKETS // 2
    exact = half // 2
    n = jnp.abs(rel)
    large = exact + (jnp.log(jnp.maximum(n, 1).astype(F32) / exact)
                     / math.log(T5_MAX_DISTANCE / exact) * (half - exact)).astype(jnp.int32)
    large = jnp.minimum(large, half - 1)
    return jnp.where(rel > 0, half, 0) + jnp.where(n < exact, n, large)


def _band_buckets(w, dilation, kw, qb):
    kk = jnp.arange(kw)[:, None]
    qq = jnp.arange(qb)[None, :]
    out = []
    for shift in (0, BAND_HALO, kw - qb):
        rel = kk - shift - qq
        out.append(jnp.where(jnp.abs(rel) <= w, _t5_bucket(dilation * rel), -1))
    return jnp.stack(out).astype(jnp.int32)


def _bias_kernel(bucket_ref, tab_ref, o_ref, *, groups):
    h = pl.program_id(0)
    bk = bucket_ref[...]
    qb = bk.shape[2]
    for g in range(groups):
        acc = jnp.zeros(bk.shape, F32)
        for b in range(NUM_BUCKETS):
            acc = jnp.where(bk == b, tab_ref[b, h * groups + g], acc)
        o_ref[:, 0, :, g * qb:(g + 1) * qb] = jnp.where(bk < 0, NEG, acc * LOG2E)


def _band_bias(table, w, dilation, seq, qb, kv_heads):
    groups = N_HEADS // kv_heads
    kw = min(seq, qb + 2 * BAND_HALO)
    buckets = _band_buckets(w, dilation, kw, qb)
    return pl.pallas_call(
        functools.partial(_bias_kernel, groups=groups),
        grid=(kv_heads,),
        in_specs=[pl.BlockSpec((3, kw, qb), lambda h: (0, 0, 0)),
                  pl.BlockSpec(memory_space=pltpu.SMEM)],
        out_specs=pl.BlockSpec((3, 1, kw, groups * qb), lambda h: (0, h, 0, 0)),
        out_shape=jax.ShapeDtypeStruct((3, kv_heads, kw, groups * qb), F32),
        compiler_params=_cparams(("parallel",)),
        name="relative_bias_tables",
    )(buckets, table)


def _banded_kernel(q_ref, k_ref, v_ref, tab_ref, *rest, groups, heads, kw, has_sink):
    if has_sink:
        sink_ref, o_ref, lse_ref = rest
    else:
        o_ref, lse_ref = rest
    nb, _, tq = q_ref.shape
    seq = k_ref.shape[2]
    t0 = pl.program_id(2) * tq
    qb = tab_ref.shape[3] // groups

    def window(j):
        qoff = pl.multiple_of(j * qb, qb)
        q0 = t0 + qoff
        sel = jnp.where(q0 == 0, 0, jnp.where(q0 == seq - qb, 2, 1))
        ks = pl.multiple_of(jnp.clip(q0 - BAND_HALO, 0, seq - kw), BAND_HALO)
        return qoff, sel, ks

    def scores(bb, j, hb):
        qoff, sel, ks = window(j)
        k_t = k_ref[bb, hb * HEAD_DIM:(hb + 1) * HEAD_DIM, pl.ds(ks, kw)]
        r0 = hb * groups * HEAD_DIM
        q_cat = jnp.concatenate(
            [q_ref[bb, r0 + g * HEAD_DIM:r0 + (g + 1) * HEAD_DIM, pl.ds(qoff, qb)]
             for g in range(groups)], axis=1)
        return _tn_dot(k_t, q_cat) + tab_ref[sel, hb]

    ones = jnp.ones((V_EXT - HEAD_DIM, kw), BF16)

    def finish(bb, j, hb, s):
        qoff, _, ks = window(j)
        v_ext = jnp.concatenate([v_ref[bb, hb * HEAD_DIM:(hb + 1) * HEAD_DIM, pl.ds(ks, kw)], ones], axis=0)
        m = jnp.max(s, axis=0, keepdims=True)
        if has_sink:
            sink = sink_ref[hb]
            m = jnp.maximum(m, sink)
        p = jnp.exp2(s - m).astype(BF16)
        ol = jnp.dot(v_ext, p, preferred_element_type=F32)
        l = ol[HEAD_DIM:HEAD_DIM + 1]
        if has_sink:
            l = l + jnp.exp2(sink - m)
        return qoff, (ol[:HEAD_DIM] / l).astype(o_ref.dtype), m + jnp.log2(l)

    nsub = tq // qb
    per_body = max(1, min(nsub, BAND_UNITS // heads))
    assert nsub % per_body == 0
    ngroups = nsub // per_body

    def body(idx, carry):
        bb = idx // ngroups
        jj = idx % ngroups
        units = [(bb, jj * per_body + u, hb) for u in range(per_body) for hb in range(heads)]
        s_vals = [scores(*unit) for unit in units[:BAND_AHEAD]]
        done = []
        for i, unit in enumerate(units):
            if i + BAND_AHEAD < len(units):
                s_vals.append(scores(*units[i + BAND_AHEAD]))
            done.append(finish(*unit, s_vals[i]))
        for (_, _, hb), (qoff, o, lse) in zip(units, done):
            r0 = hb * groups * HEAD_DIM
            for g in range(groups):
                o_ref[bb, r0 + g * HEAD_DIM:r0 + (g + 1) * HEAD_DIM, pl.ds(qoff, qb)] = o[:, g * qb:(g + 1) * qb]
                lse_ref[bb, hb, g:g + 1, pl.ds(qoff, qb)] = lse[:, g * qb:(g + 1) * qb]
        return carry

    lax.fori_loop(0, nb * ngroups, body, 0)


def _largest_divisor(n, cap, ok=lambda d: True):
    return max(d for d in range(1, n + 1) if n % d == 0 and d <= max(cap, 1) and ok(d))


def _banded_attention(q, k, v, bias, sink, *, kv_heads, q_row, k_row, v_row):
    b, _, seq = q.shape
    groups = N_HEADS // kv_heads
    kw = bias.shape[2]
    tq = min(seq, BAND_TQ)
    def aligned(d):
        return q_row % (d * groups * HEAD_DIM) == 0 and k_row % (d * HEAD_DIM) == 0 and v_row % (d * HEAD_DIM) == 0

    per_head = HEAD_DIM * (2 * groups * tq + 2 * seq)
    hb = _largest_divisor(kv_heads, BAND_STEP_ELEMS // per_head, aligned)
    nb = _largest_divisor(b, BAND_STEP_ELEMS // (per_head * hb))
    qrows, krows = hb * groups * HEAD_DIM, hb * HEAD_DIM
    in_specs = [pl.BlockSpec((nb, qrows, tq), lambda i, h, t: (i, q_row // qrows + h, t)),
                pl.BlockSpec((nb, krows, seq), lambda i, h, t: (i, k_row // krows + h, 0)),
                pl.BlockSpec((nb, krows, seq), lambda i, h, t: (i, v_row // krows + h, 0)),
                pl.BlockSpec((3, hb, kw, bias.shape[3]), lambda i, h, t: (0, h, 0, 0))]
    args = [q, k, v, bias]
    if sink is not None:
        in_specs.append(pl.BlockSpec((hb, 1, bias.shape[3]), lambda i, h, t: (h, 0, 0)))
        args.append(sink)
    return pl.pallas_call(
        functools.partial(_banded_kernel, groups=groups, heads=hb, kw=kw, has_sink=sink is not None),
        grid=(b // nb, kv_heads // hb, seq // tq),
        in_specs=in_specs,
        out_specs=[pl.BlockSpec((nb, qrows, tq), lambda i, h, t: (i, h, t)),
                   pl.BlockSpec((nb, hb, groups, tq), lambda i, h, t: (i, h, 0, t))],
        out_shape=[jax.ShapeDtypeStruct((b, GROUP_WIDTH, seq), BF16),
                   jax.ShapeDtypeStruct((b, kv_heads, groups, seq), F32)],
        compiler_params=_cparams(("parallel", "parallel", "arbitrary")),
        name="banded_attention",
    )(*args)


def _dilated_mixture(o_refs, l_refs):
    heads = []
    for h in range(N_HEADS):
        a1, a2, a3 = (l_ref[0, h] for l_ref in l_refs)
        mx = jnp.maximum(jnp.maximum(a1, a2), a3)
        e1, e2, e3 = jnp.exp2(a1 - mx), jnp.exp2(a2 - mx), jnp.exp2(a3 - mx)
        inv = 1.0 / (e1 + e2 + e3)
        rows = slice(h * HEAD_DIM, (h + 1) * HEAD_DIM)
        o1, o2, o3 = (o_ref[0, rows, :].astype(F32) for o_ref in o_refs)
        heads.append((e1 * inv) * o1 + (e2 * inv) * o2 + (e3 * inv) * o3)
    return jnp.concatenate(heads, axis=0)


def _outproj_kernel(ya_ref, yb_ref, o1_ref, o2_ref, o3_ref, l1_ref, l2_ref, l3_ref, yd_ref,
                    z_ref, x_ref, gate_ref, w_ref, g_ref, b_ref, o_ref):
    y_c = _dilated_mixture((o1_ref, o2_ref, o3_ref), (l1_ref, l2_ref, l3_ref))
    groups = (ya_ref[0].astype(F32), yb_ref[0].astype(F32), y_c, yd_ref[0].astype(F32))
    acc = None
    for i, y in enumerate(groups):
        rows = slice(i * GROUP_WIDTH, (i + 1) * GROUP_WIDTH)
        z = z_ref[0, rows, :].astype(F32)
        gated = (y * _silu(z)).astype(BF16)
        part = _tn_dot(gated, w_ref[rows, :])
        acc = part if acc is None else acc + part
    r = ALPHA * x_ref[0] + gate_ref[0] * acc
    mu = jnp.mean(r, axis=-1, keepdims=True)
    rc = r - mu
    var = jnp.mean(rc * rc, axis=-1, keepdims=True)
    o_ref[0] = (rc * lax.rsqrt(var + EPS)) * g_ref[...] + b_ref[...]


def _output_projection(y_a, y_b, outs_c, lses_c, y_d, proj, x, mod, w_out, ln_g, ln_b, tm):
    b, s, _ = x.shape
    yspec = pl.BlockSpec((1, GROUP_WIDTH, tm), lambda i, t: (i, 0, t))
    lspec = pl.BlockSpec((1, N_HEADS, 1, tm), lambda i, t: (i, 0, 0, t))
    return pl.pallas_call(
        _outproj_kernel,
        grid=(b, s // tm),
        in_specs=[yspec] * 5 + [lspec] * 3 + [yspec] + [
            pl.BlockSpec((1, D_MIX, tm), lambda i, t: (i, Z_ROW // D_MIX, t)),
            pl.BlockSpec((1, tm, D_MODEL), lambda i, t: (i, t, 0)),
            pl.BlockSpec((1, 1, D_MODEL), lambda i, t: (i, 0, 2)),
            pl.BlockSpec((D_MIX, D_MODEL), lambda i, t: (0, 0)),
            pl.BlockSpec((1, D_MODEL), lambda i, t: (0, 0)),
            pl.BlockSpec((1, D_MODEL), lambda i, t: (0, 0))],
        out_specs=pl.BlockSpec((1, tm, D_MODEL), lambda i, t: (i, t, 0)),
        out_shape=jax.ShapeDtypeStruct((b, s, D_MODEL), F32),
        compiler_params=_cparams(("parallel", "parallel")),
        name="output_projection",
    )(y_a, y_b, *outs_c, *lses_c, y_d, proj, x, mod, w_out, ln_g, ln_b)


def _rope_tables(seq):
    pos = jnp.arange(seq)
    inv = ROPE_THETA ** (-jnp.arange(0, A_ROPE, 2, dtype=F32) / A_ROPE)

    def cs(p):
        ang = p.astype(F32)[:, None] * inv[None, :]
        return jnp.cos(ang).T, jnp.sin(ang).T

    c, sn = cs(pos)
    ones, zeros = jnp.ones((A_NOPE, seq), F32), jnp.zeros((A_NOPE, seq), F32)
    pad = jnp.zeros((A_QK_PAD - A_NOPE - A_ROPE, seq), F32)
    a_c = jnp.concatenate([ones, c, c, pad], axis=0)
    a_s = jnp.concatenate([zeros, -sn, sn, pad], axis=0)
    cr, sr = cs(pos // GRID_W)
    cc, sc = cs(pos % GRID_W)
    d_c = jnp.concatenate([cr, cr, cc, cc], axis=0)
    d_s = jnp.concatenate([-sr, sr, -sc, sc], axis=0)
    return a_c, a_s, d_c, d_s


def _layer_weights(l, w_in, a_q_norm, a_q_up, a_kv_norm, a_kv_up, b_sink, d_q_norm, d_k_norm, w_out,
                   ln_g, ln_b):
    wa, wb, wc, wd, wz = jnp.split(w_in[l], np.cumsum([A_COLS, B_COLS, C_COLS, D_COLS]).tolist(), axis=1)
    wa = jnp.pad(wa, ((0, 0), (0, A_ROWS - A_COLS)))
    band_scale = jnp.concatenate([jnp.full((GROUP_WIDTH,), HEAD_DIM ** -0.5 * LOG2E, F32),
                                  jnp.ones((B_COLS - GROUP_WIDTH,), F32)])
    wb = wb * band_scale[None, :]
    wc = wc * jnp.pad(band_scale[:GROUP_WIDTH], (0, C_COLS - GROUP_WIDTH), constant_values=1.0)[None, :]
    w_t = jnp.concatenate([wz, wa, wb, wd, wc], axis=1).T.astype(BF16)
    dqk = A_NOPE + A_ROPE
    wq = jnp.pad(a_q_up[l].reshape(A_Q_LORA, N_HEADS, dqk), ((0, 0), (0, 0), (0, A_QK_PAD - dqk)))
    wq_t = wq.reshape(A_Q_LORA, N_HEADS * A_QK_PAD).T.astype(BF16)
    kvu = a_kv_up[l].reshape(A_KV_LORA, N_HEADS, A_NOPE + HEAD_DIM)
    wk_top = jnp.pad(kvu[:, :, :A_NOPE], ((0, 0), (0, 0), (0, A_QK_PAD - A_NOPE)))
    sel = jnp.zeros((A_ROPE, N_HEADS, A_QK_PAD), F32)
    sel = sel.at[:, :, A_NOPE:A_NOPE + A_ROPE].set(jnp.eye(A_ROPE, dtype=F32)[:, None, :])
    wk_t = jnp.concatenate([wk_top, sel], axis=0).reshape(A_KV_LORA + A_ROPE, -1).T.astype(BF16)
    wv_t = kvu[:, :, A_NOPE:].reshape(A_KV_LORA, GROUP_WIDTH).T.astype(BF16)
    d_gain = jnp.concatenate([jnp.broadcast_to(d_q_norm[l], (N_HEADS, HEAD_DIM)),
                              jnp.broadcast_to(d_k_norm[l], (D_KV_HEADS, HEAD_DIM))])[:, :, None]
    groups = N_HEADS // B_KV_HEADS
    sink = jnp.repeat(b_sink[l].reshape(B_KV_HEADS, 1, groups) * LOG2E, BAND_QB, axis=2)
    return dict(w_t=w_t, gq=a_q_norm[l][:, None], gkv=a_kv_norm[l][:, None], wq_t=wq_t, wk_t=wk_t,
                wv_t=wv_t, d_gain=d_gain, sink=sink, w_out=w_out[l].astype(BF16),
                ln_g=ln_g[l][None, :], ln_b=ln_b[l][None, :])


def _to_strided(t, r):
    b, f, s = t.shape
    return t.reshape(b, f, s // r, r).transpose(0, 3, 1, 2).reshape(b * r, f, s // r)


def _from_strided(t, b, r):
    rest = t.shape[1:-1]
    sr = t.shape[-1]
    t = jnp.moveaxis(t.reshape((b, r) + rest + (sr,)), 1, -1)
    return t.reshape((b,) + rest + (sr * r,))


def _pick(n, pref):
    return min(n, pref)


def _layer(x, mod, wts, bias_b, bias_c, ropes):
    b, s, _ = x.shape
    a_c, a_s, d_c, d_s = ropes
    proj = _input_projection(x, mod, wts["w_t"], tm=_pick(s, 1024), tn=1088)

    tp = _pick(s, 512)
    q_a, k_a, v_a = _mla_prep(proj, wts["gq"], wts["gkv"], wts["wq_t"], wts["wk_t"], wts["wv_t"],
                              a_c, a_s, tp)
    short = s <= 2048
    y_a = _dense_attention(q_a, k_a, v_a, kv_heads=N_HEADS, dq=A_QK_PAD, k_row=0, v_row=0,
                           heads=1, nsub=4, chunks=8, tk=256 if short else 512)

    y_b, _ = _banded_attention(proj, proj, proj, bias_b, wts["sink"], kv_heads=B_KV_HEADS,
                               q_row=B_ROW, k_row=B_ROW + GROUP_WIDTH,
                               v_row=B_ROW + GROUP_WIDTH + B_KV_HEADS * HEAD_DIM)

    outs, lses = [], []
    for (window, r), bias in zip(DILATED_PAIRS, bias_c):
        if r == 1:
            o, lse = _banded_attention(proj, proj, proj, bias, None, kv_heads=N_HEADS,
                                       q_row=C_ROW, k_row=C_ROW + GROUP_WIDTH, v_row=C_ROW + 2 * GROUP_WIDTH)
        else:
            qkv = _to_strided(proj[:, C_ROW:, :], r)
            o, lse = _banded_attention(qkv, qkv, qkv, bias, None, kv_heads=N_HEADS,
                                       q_row=0, k_row=GROUP_WIDTH, v_row=2 * GROUP_WIDTH)
            o, lse = _from_strided(o, b, r), _from_strided(lse, b, r)
        outs.append(o)
        lses.append(lse)

    q_d, k_d = _axial_prep(proj, wts["d_gain"], d_c, d_s, tp)
    y_d = _dense_attention(q_d, k_d, proj, kv_heads=D_KV_HEADS, dq=HEAD_DIM, k_row=0,
                           v_row=D_ROW + GROUP_WIDTH + D_KV_HEADS * HEAD_DIM,
                           heads=1, nsub=2, chunks=8, tk=256 if short else 512)

    return _output_projection(y_a, y_b, outs, lses, y_d, proj, x, mod, wts["w_out"], wts["ln_g"], wts["ln_b"],
                              tm=_pick(s, 256))


def _trunk(x, mods, layer_wts, rel_bias):
    b, s, _ = x.shape
    ropes = _rope_tables(s)
    bias_b = _band_bias(rel_bias[:, :N_HEADS], B_WINDOW, 1, s, BAND_QB, B_KV_HEADS)
    bias_c = [_band_bias(rel_bias[:, N_HEADS:], window // (2 * r), r, s // r, BAND_QB, N_HEADS)
              for window, r in DILATED_PAIRS]
    for l in range(DEPTH):
        x = _layer(x, mods[l], layer_wts[l], bias_b, bias_c, ropes)
    return x


def kernel(x_prompt, x_sample, c_prompt, c_sample, w_ada, b_ada, w_in, a_q_norm, a_q_up, a_kv_norm, a_kv_up, b_sink, d_q_norm, d_k_norm, w_out, ln_g, ln_b, rel_bias):
    bp, bs = c_prompt.shape[0], c_sample.shape[0]
    rows = -(-(bp + bs) // 16) * 16
    c_all = jnp.pad(jnp.concatenate([c_prompt, c_sample], axis=0), ((0, rows - bp - bs), (0, 0)))
    mod = _modulation(c_all, w_ada, b_ada)
    layer_wts = [_layer_weights(l, w_in, a_q_norm, a_q_up, a_kv_norm, a_kv_up, b_sink, d_q_norm,
                                d_k_norm, w_out, ln_g, ln_b) for l in range(DEPTH)]
    mods_p = [mod[l, :bp].reshape(bp, 1, -1) for l in range(DEPTH)]
    mods_s = [mod[l, bp:bp + bs].reshape(bs, 1, -1) for l in range(DEPTH)]
    y_prompt = _trunk(x_prompt, mods_p, layer_wts, rel_bias)
    y_sample = _trunk(x_sample, mods_s, layer_wts, rel_bias)
    return (y_prompt, y_sample)
```
